```python
import math
import jax, jax.numpy as jnp
from jax import lax
import numpy as np

D_MODEL = 4096
BATCH = 2
SEQ = 4096
DEPTH = 1
DEC_BATCH = 32
DEC_SEQ = 8
PAST_LEN = 8192
PAGE_SIZE = 128

HEAD_DIM = 128
V_DIM = 2 * HEAD_DIM
ATT_WIDTH = D_MODEL // 2
N_HEADS = ATT_WIDTH // V_DIM
QK_WIDTH = N_HEADS * 2 * HEAD_DIM
CONV_CH = D_MODEL - ATT_WIDTH
CONV_WIDTH = 31
D_FF = 4 * D_MODEL
Q_BLOCK = 128
EPS = 1e-6
SCALE = HEAD_DIM ** -0.5

kernel_name = 'hymba_diffattn_conformer_decode_step'


def _rms(x, g):
    xf = x.astype(jnp.float32)
    y = xf * lax.rsqrt(jnp.mean(xf * xf, axis=-1, keepdims=True) + EPS)
    return (y * g.astype(jnp.float32)).astype(x.dtype)


def _alibi_slopes():
    return jnp.exp2(-8.0 * jnp.arange(1, N_HEADS + 1, dtype=jnp.float32) / N_HEADS)


def _lambda_init(layer):
    return 0.8 - 0.6 * math.exp(-0.3 * layer)


def _diff_attention(q, q_pos, segs, lam):
    slopes = _alibi_slopes()
    scores = []
    for k, v, k_pos in segs:
        s = jnp.einsum('bqhcd,bkhcd->bhcqk', q, k).astype(jnp.float32) * SCALE
        dist = (q_pos[:, None] - k_pos[None, :]).astype(jnp.float32)
        s = s - slopes[None, :, None, None, None] * dist[None, None, None]
        scores.append(jnp.where((dist >= 0)[None, None, None], s, -jnp.inf))
    p = jax.nn.softmax(jnp.concatenate(scores, axis=-1), axis=-1)
    a = p[:, :, 0] - lam * p[:, :, 1]
    out = None
    start = 0
    for k, v, k_pos in segs:
        n = k.shape[1]
        o = jnp.einsum('bhqk,bkhe->bqhe', a[..., start:start + n].astype(v.dtype), v)
        out = o if out is None else out + o
        start += n
    return out


def _prompt_attention(q, k, v, lam):
    B, S = q.shape[:2]
    nb = S // Q_BLOCK
    qb = jnp.moveaxis(q.reshape(B, nb, Q_BLOCK, N_HEADS, 2, HEAD_DIM), 1, 0)
    pos = jnp.arange(S, dtype=jnp.int32)
    pb = pos.reshape(nb, Q_BLOCK)

    def blk(args):
        qi, pi = args
        return _diff_attention(qi, pi, ((k, v, pos),), lam)

    o = lax.map(blk, (qb, pb))
    return jnp.moveaxis(o, 0, 1).reshape(B, S, N_HEADS, V_DIM)


def _sample_attention(q, k_new, v_new, cache_k, cache_v, page_table, lam):
    T = q.shape[1]
    past = page_table.shape[1] * cache_k.shape[1]
    past_pos = jnp.arange(past, dtype=jnp.int32)
    new_pos = past + jnp.arange(T, dtype=jnp.int32)

    def one(args):
        qi, ki, vi, pt = args
        kp = cache_k[pt].reshape(1, past, N_HEADS, 2, HEAD_DIM)
        vp = cache_v[pt].reshape(1, past, N_HEADS, V_DIM)
        segs = ((kp, vp, past_pos), (ki[None], vi[None], new_pos))
        return _diff_attention(qi[None], new_pos, segs, lam)[0]

    return lax.map(one, (q, k_new, v_new, page_table))


def _conv_module(ca, cg, prefix, w_dw, b_dw, ln_g, ln_b):
    u = ca * jax.nn.sigmoid(cg)
    ext = jnp.concatenate([prefix.astype(u.dtype), u], axis=1)
    y = lax.conv_general_dilated(ext, w_dw[:, None, :], window_strides=(1,), padding='VALID',
                                 dimension_numbers=('NWC', 'WIO', 'NWC'),
                                 feature_group_count=CONV_CH) + b_dw
    yf = y.astype(jnp.float32)
    mu = jnp.mean(yf, axis=-1, keepdims=True)
    var = jnp.mean(jnp.square(yf - mu), axis=-1, keepdims=True)
    yn = (yf - mu) * lax.rsqrt(var + EPS) * ln_g.astype(jnp.float32) + ln_b.astype(jnp.float32)
    return jax.nn.silu(yn).astype(u.dtype), ext[:, -(CONV_WIDTH - 1):]


def _block(x, p, layer, attend, conv_prefix):
    B, T = x.shape[:2]
    lam_init = _lambda_init(layer)
    lam = (jnp.exp(jnp.sum(p['lq1'].astype(jnp.float32) * p['lk1'].astype(jnp.float32)))
           - jnp.exp(jnp.sum(p['lq2'].astype(jnp.float32) * p['lk2'].astype(jnp.float32)))
           + lam_init)
    h = _rms(x, p['attn_norm_g'])
    z = h @ p['w_in']
    q, k, v, ca, cg = jnp.split(z, [QK_WIDTH, 2 * QK_WIDTH, 2 * QK_WIDTH + ATT_WIDTH,
                                    2 * QK_WIDTH + ATT_WIDTH + CONV_CH], axis=-1)
    q = _rms(q.reshape(B, T, N_HEADS, 2, HEAD_DIM), p['q_norm_g'])
    k = _rms(k.reshape(B, T, N_HEADS, 2, HEAD_DIM), p['k_norm_g'])
    v = v.reshape(B, T, N_HEADS, V_DIM)
    o = attend(q, k, v, lam)
    att = (_rms(o, p['subln_g']) * (1.0 - lam_init)).reshape(B, T, ATT_WIDTH)
    cv, conv_state = _conv_module(ca, cg, conv_prefix, p['conv_w'], p['conv_b'],
                                  p['conv_ln_g'], p['conv_ln_b'])
    x = x + jnp.concatenate([att.astype(x.dtype), cv], axis=-1) @ p['w_out']
    h2 = _rms(x, p['mlp_norm_g'])
    x = x + jnp.square(jax.nn.relu(h2 @ p['w_up'])) @ p['w_down']
    return x, k.reshape(B, T, N_HEADS, 2 * HEAD_DIM), v, conv_state


def setup_inputs(seed: int = 0) -> dict:
    key = jax.random.key(seed)
    ks = jax.random.split(key, 24)
    f32 = jnp.float32
    n_pages = PAST_LEN // PAGE_SIZE
    n_used = DEC_BATCH * n_pages
    n_phys = n_used + (n_used + 3) // 4
    nrm = lambda k, s, sc: jax.random.normal(k, s, f32) * sc
    page_table = jax.random.permutation(ks[5], n_phys)[:n_used].reshape(DEC_BATCH, n_pages).astype(jnp.int32)
    return {
        'x_prompt': nrm(ks[0], (BATCH, SEQ, D_MODEL), 1.0),
        'x_sample': nrm(ks[1], (DEC_BATCH, DEC_SEQ, D_MODEL), 1.0),
        'cache_k': nrm(ks[2], (DEPTH, n_phys, PAGE_SIZE, N_HEADS, 2 * HEAD_DIM), 1.0),
        'cache_v': nrm(ks[3], (DEPTH, n_phys, PAGE_SIZE, N_HEADS, V_DIM), 1.0),
        'state_conv': nrm(ks[4], (DEPTH, DEC_BATCH, CONV_WIDTH - 1, CONV_CH), 1.0),
        'page_table': page_table,
        'attn_norm_g': 1.0 + nrm(ks[6], (DEPTH, D_MODEL), 0.02),
        'w_in': nrm(ks[7], (DEPTH, D_MODEL, 2 * QK_WIDTH + ATT_WIDTH + 2 * CONV_CH), D_MODEL ** -0.5),
        'q_norm_g': 1.0 + nrm(ks[8], (DEPTH, HEAD_DIM), 0.02),
        'k_norm_g': 1.0 + nrm(ks[9], (DEPTH, HEAD_DIM), 0.02),
        'lambda_q1': nrm(ks[10], (DEPTH, HEAD_DIM), 0.1),
        'lambda_k1': nrm(ks[11], (DEPTH, HEAD_DIM), 0.1),
        'lambda_q2': nrm(ks[12], (DEPTH, HEAD_DIM), 0.1),
        'lambda_k2': nrm(ks[13], (DEPTH, HEAD_DIM), 0.1),
        'subln_g': 1.0 + nrm(ks[14], (DEPTH, V_DIM), 0.02),
        'conv_w': nrm(ks[15], (DEPTH, CONV_WIDTH, CONV_CH), CONV_WIDTH ** -0.5),
        'conv_b': nrm(ks[16], (DEPTH, CONV_CH), 0.02),
        'conv_ln_g': 1.0 + nrm(ks[17], (DEPTH, CONV_CH), 0.02),
        'conv_ln_b': nrm(ks[18], (DEPTH, CONV_CH), 0.02),
        'w_out': nrm(ks[19], (DEPTH, D_MODEL, D_MODEL), D_MODEL ** -0.5),
        'mlp_norm_g': 1.0 + nrm(ks[20], (DEPTH, D_MODEL), 0.02),
        'w_up': nrm(ks[21], (DEPTH, D_MODEL, D_FF), D_MODEL ** -0.5),
        'w_down': nrm(ks[22], (DEPTH, D_FF, D_MODEL), D_FF ** -0.5),
    }


def reference(x_prompt, x_sample, cache_k, cache_v, state_conv, page_table,
              attn_norm_g, w_in, q_norm_g, k_norm_g, lambda_q1, lambda_k1, lambda_q2, lambda_k2,
              subln_g, conv_w, conv_b, conv_ln_g, conv_ln_b, w_out, mlp_norm_g, w_up, w_down):
    xp, xs = x_prompt, x_sample
    kp_l, vp_l, cp_l, ks_l, vs_l, cs_l = [], [], [], [], [], []
    for l in range(DEPTH):
        p = {'attn_norm_g': attn_norm_g[l], 'w_in': w_in[l], 'q_norm_g': q_norm_g[l],
             'k_norm_g': k_norm_g[l], 'lq1': lambda_q1[l], 'lk1': lambda_k1[l],
             'lq2': lambda_q2[l], 'lk2': lambda_k2[l], 'subln_g': subln_g[l],
             'conv_w': conv_w[l], 'conv_b': conv_b[l], 'conv_ln_g': conv_ln_g[l],
             'conv_ln_b': conv_ln_b[l], 'w_out': w_out[l], 'mlp_norm_g': mlp_norm_g[l],
             'w_up': w_up[l], 'w_down': w_down[l]}
        prompt_prefix = jnp.zeros((xp.shape[0], CONV_WIDTH - 1, CONV_CH), xp.dtype)
        xp, kp, vp, cp = _block(xp, p, l, _prompt_attention, prompt_prefix)
        ck, cvv = cache_k[l], cache_v[l]
        attend_s = lambda q, k, v, lam: _sample_attention(q, k, v, ck, cvv, page_table, lam)
        xs, ksn, vsn, csn = _block(xs, p, l, attend_s, state_conv[l])
        kp_l.append(kp); vp_l.append(vp); cp_l.append(cp)
        ks_l.append(ksn); vs_l.append(vsn); cs_l.append(csn)
    return (xp, xs, jnp.stack(kp_l), jnp.stack(vp_l), jnp.stack(cp_l),
            jnp.stack(ks_l), jnp.stack(vs_l), jnp.stack(cs_l))
```

```python
import functools
import math

import jax
import jax.numpy as jnp
from jax import lax
from jax.experimental import pallas as pl
from jax.experimental.pallas import tpu as pltpu

HEAD_DIM = 128
V_DIM = 2 * HEAD_DIM
CONV_WIDTH = 31
EPS = 1e-6
SCALE = HEAD_DIM ** -0.5
NEG = -1e30
M_FLOOR = -1e20
LANES = 128
SUBLANES = 8
VMEM_LIMIT = 56 * 1024 * 1024
HALO = 32


def _params(*sem):
    return pltpu.CompilerParams(dimension_semantics=sem, vmem_limit_bytes=VMEM_LIMIT)


def _rms_cast_kernel(x_ref, g_ref, o_ref):
    x = x_ref[...]
    ms = jnp.mean(x * x, axis=-1, keepdims=True)
    o_ref[...] = ((x * lax.rsqrt(ms + EPS)) * g_ref[...]).astype(o_ref.dtype)


def _rms_cast(x, g, tm):
    m, d = x.shape
    return pl.pallas_call(
        _rms_cast_kernel,
        grid=(m // tm,),
        in_specs=[pl.BlockSpec((tm, d), lambda i: (i, 0)),
                  pl.BlockSpec((1, d), lambda i: (0, 0))],
        out_specs=pl.BlockSpec((tm, d), lambda i: (i, 0)),
        out_shape=jax.ShapeDtypeStruct((m, d), jnp.bfloat16),
        compiler_params=_params("parallel"),
    )(x, g.reshape(1, d))


def _group_rms(acc, g):
    outs = []
    for c in range(acc.shape[1] // HEAD_DIM):
        xg = acc[:, c * HEAD_DIM:(c + 1) * HEAD_DIM]
        ms = jnp.mean(xg * xg, axis=-1, keepdims=True)
        outs.append(xg * lax.rsqrt(ms + EPS))
    return jnp.concatenate(outs, axis=-1) * g


def _proj_norm_kernel(h_ref, w_ref, g_ref, *o_refs):
    acc = jnp.dot(h_ref[...], w_ref[...], preferred_element_type=jnp.float32)
    y = _group_rms(acc, g_ref[...])
    for o_ref in o_refs:
        o_ref[...] = y.astype(o_ref.dtype)


def _proj_plain_kernel(h_ref, w_ref, *o_refs):
    acc = jnp.dot(h_ref[...], w_ref[...], preferred_element_type=jnp.float32)
    for o_ref in o_refs:
        o_ref[...] = acc.astype(o_ref.dtype)


def _proj_glu_kernel(h_ref, wa_ref, wg_ref, o_ref):
    h = h_ref[...]
    a = jnp.dot(h, wa_ref[...], preferred_element_type=jnp.float32)
    g = jnp.dot(h, wg_ref[...], preferred_element_type=jnp.float32)
    o_ref[...] = a * jax.nn.sigmoid(g)


def _proj(kernel_fn, h, w, col_offs, n_cols, out_dtypes, tm, tn, gain=None):
    m, k = h.shape
    in_specs = [pl.BlockSpec((tm, k), lambda i, j: (i, 0))]
    args = [h]
    for off in col_offs:
        in_specs.append(pl.BlockSpec((k, tn), lambda i, j, o=off // tn: (0, o + j)))
        args.append(w)
    if gain is not None:
        in_specs.append(pl.BlockSpec((1, tn), lambda i, j: (0, j)))
        args.append(gain)
    out = pl.pallas_call(
        kernel_fn,
        grid=(m // tm, n_cols // tn),
        in_specs=in_specs,
        out_specs=[pl.BlockSpec((tm, tn), lambda i, j: (i, j)) for _ in out_dtypes],
        out_shape=[jax.ShapeDtypeStruct((m, n_cols), dt) for dt in out_dtypes],
        compiler_params=_params("parallel", "arbitrary"),
    )(*args)
    return out


def _conv_kernel(halo_ref, u_ref, w_ref, b_ref, lg_ref, lb_ref, o_ref, ext_s, y_s,
                 *, zero_first, cchunk):
    tt, ch = u_ref.shape[1], u_ref.shape[2]
    ext_s[pl.ds(0, HALO), :] = halo_ref[0]
    if zero_first:
        @pl.when(pl.program_id(1) == 0)
        def _():
            ext_s[pl.ds(0, HALO), :] = jnp.zeros((HALO, ch), jnp.float32)
    ext_s[pl.ds(HALO, tt), :] = u_ref[0]
    first = HALO - (CONV_WIDTH - 1)

    def chunk(c, carry):
        c0 = pl.multiple_of(c * cchunk, cchunk)
        acc = jnp.broadcast_to(b_ref[:, pl.ds(c0, cchunk)], (tt, cchunk))
        for tap in range(CONV_WIDTH):
            acc = acc + ext_s[pl.ds(first + tap, tt), pl.ds(c0, cchunk)] * w_ref[pl.ds(tap, 1), pl.ds(c0, cchunk)]
        y_s[:, pl.ds(c0, cchunk)] = acc
        return carry

    lax.fori_loop(0, ch // cchunk, chunk, 0)
    y = y_s[...]
    mu = jnp.mean(y, axis=-1, keepdims=True)
    d = y - mu
    var = jnp.mean(d * d, axis=-1, keepdims=True)
    yn = d * lax.rsqrt(var + EPS) * lg_ref[...] + lb_ref[...]
    o_ref[0] = (yn * jax.nn.sigmoid(yn)).astype(o_ref.dtype)


def _conv_module(u, halo, halo_blocks_per_tile, zero_first, conv_w, conv_b, ln_g, ln_b, tt):
    bsz, t, ch = u.shape
    hb = halo_blocks_per_tile
    kern = functools.partial(_conv_kernel, zero_first=zero_first, cchunk=2 * LANES)
    return pl.pallas_call(
        kern,
        grid=(bsz, t // tt),
        in_specs=[pl.BlockSpec((1, HALO, ch), lambda b, i: (b, jnp.maximum(i * hb - 1, 0), 0)),
                  pl.BlockSpec((1, tt, ch), lambda b, i: (b, i, 0)),
                  pl.BlockSpec((CONV_WIDTH, ch), lambda b, i: (0, 0)),
                  pl.BlockSpec((1, ch), lambda b, i: (0, 0)),
                  pl.BlockSpec((1, ch), lambda b, i: (0, 0)),
                  pl.BlockSpec((1, ch), lambda b, i: (0, 0))],
        out_specs=pl.BlockSpec((1, tt, ch), lambda b, i: (b, i, 0)),
        out_shape=jax.ShapeDtypeStruct((bsz, t, ch), jnp.bfloat16),
        scratch_shapes=[pltpu.VMEM((HALO + tt, ch), jnp.float32),
                        pltpu.VMEM((tt, ch), jnp.float32)],
        compiler_params=_params("parallel", "arbitrary"),
    )(halo, u, conv_w, conv_b.reshape(1, ch), ln_g.reshape(1, ch), ln_b.reshape(1, ch))


def _lambda_full(lq1, lk1, lq2, lk2, lam_init):
    return (jnp.exp(jnp.sum(lq1 * lk1, axis=-1, keepdims=True))
            - jnp.exp(jnp.sum(lq2 * lk2, axis=-1, keepdims=True)) + lam_init)


def _subln(o, g, lam_init):
    ms = jnp.mean(o * o, axis=-1, keepdims=True)
    return (o * lax.rsqrt(ms + EPS)) * g * (1.0 - lam_init)


def _prompt_attn_kernel(qi_ref, ki_ref, slope_ref, q_ref, k_ref, v_ref, rel_ref,
                        lq1_ref, lk1_ref, lq2_ref, lk2_ref, sg_ref, o_ref,
                        m_s, l_s, acc_s, *, lam_init):
    h = pl.program_id(1)
    t = pl.program_id(2)
    qi = qi_ref[t]
    ki = ki_ref[t]
    tq = q_ref.shape[1]
    slope = slope_ref[h]

    @pl.when(ki == 0)
    def _():
        m_s[...] = jnp.full(m_s.shape, NEG, jnp.float32)
        l_s[...] = jnp.zeros(l_s.shape, jnp.float32)
        acc_s[...] = jnp.zeros(acc_s.shape, jnp.float32)

    q = q_ref[0]
    k = k_ref[0]
    v = v_ref[0]
    dist = rel_ref[...] + ((qi - ki) * tq).astype(jnp.float32)
    bias = slope * dist
    for c in range(2):
        s = lax.dot_general(q[:, c * HEAD_DIM:(c + 1) * HEAD_DIM], k[:, c * HEAD_DIM:(c + 1) * HEAD_DIM],
                            (((1,), (1,)), ((), ())), preferred_element_type=jnp.float32) * SCALE
        s = jnp.where(dist >= 0, s - bias, NEG)
        m_prev = m_s[c]
        m_new = jnp.maximum(m_prev, jnp.max(s, axis=-1, keepdims=True))
        alpha = jnp.exp(m_prev - m_new)
        p = jnp.exp(s - m_new)
        l_s[c] = alpha * l_s[c] + jnp.sum(p, axis=-1, keepdims=True)
        acc_s[c] = alpha * acc_s[c] + jnp.dot(p.astype(v.dtype), v, preferred_element_type=jnp.float32)
        m_s[c] = m_new

    @pl.when(ki == qi)
    def _():
        lam = _lambda_full(lq1_ref[...], lk1_ref[...], lq2_ref[...], lk2_ref[...], lam_init)
        o = acc_s[0] / l_s[0] - lam * (acc_s[1] / l_s[1])
        o_ref[0] = _subln(o, sg_ref[...], lam_init).astype(o_ref.dtype)


def _prompt_attention(q, k, v, slopes, lam_vecs, subln_g, lam_init, tq):
    bsz, s, width = q.shape
    nh = width // V_DIM
    nb = s // tq
    pairs = [(i, j) for i in range(nb) for j in range(i + 1)]
    qi_idx = jnp.asarray([p[0] for p in pairs], jnp.int32)
    ki_idx = jnp.asarray([p[1] for p in pairs], jnp.int32)
    rel = (jnp.arange(tq, dtype=jnp.int32)[:, None] - jnp.arange(tq, dtype=jnp.int32)[None, :]).astype(jnp.float32)
    vec = lambda: pl.BlockSpec((1, HEAD_DIM), lambda b, h, t, qi, ki, sl: (0, 0))
    grid_spec = pltpu.PrefetchScalarGridSpec(
        num_scalar_prefetch=3,
        grid=(bsz, nh, len(pairs)),
        in_specs=[pl.BlockSpec((1, tq, V_DIM), lambda b, h, t, qi, ki, sl: (b, qi[t], h)),
                  pl.BlockSpec((1, tq, V_DIM), lambda b, h, t, qi, ki, sl: (b, ki[t], h)),
                  pl.BlockSpec((1, tq, V_DIM), lambda b, h, t, qi, ki, sl: (b, ki[t], h)),
                  pl.BlockSpec((tq, tq), lambda b, h, t, qi, ki, sl: (0, 0)),
                  vec(), vec(), vec(), vec(),
                  pl.BlockSpec((1, V_DIM), lambda b, h, t, qi, ki, sl: (0, 0))],
        out_specs=pl.BlockSpec((1, tq, V_DIM), lambda b, h, t, qi, ki, sl: (b, qi[t], h)),
        scratch_shapes=[pltpu.VMEM((2, tq, 1), jnp.float32),
                        pltpu.VMEM((2, tq, 1), jnp.float32),
                        pltpu.VMEM((2, tq, V_DIM), jnp.float32)],
    )
    return pl.pallas_call(
        functools.partial(_prompt_attn_kernel, lam_init=lam_init),
        grid_spec=grid_spec,
        out_shape=jax.ShapeDtypeStruct((bsz, s, width), jnp.bfloat16),
        compiler_params=_params("parallel", "parallel", "arbitrary"),
    )(qi_idx, ki_idx, slopes, q, k, v, rel, *lam_vecs, subln_g.reshape(1, V_DIM))


def _diag_rows(x, diag):
    return jnp.sum(jnp.where(diag, x, 0.0), axis=0, keepdims=True)


def _sample_attn_kernel(pt_ref, *refs, n_group, n_steps, page_rows, lam_init):
    k_refs = refs[:n_group]
    v_refs = refs[n_group:2 * n_group]
    (qt_ref, kn_ref, vn_ref, bias_ref, biasn_ref, coff_ref,
     lq1_ref, lk1_ref, lq2_ref, lk2_ref, sg_ref, o_ref, m_s, l_s, o_s) = refs[2 * n_group:]
    step = pl.program_id(1)
    qt = qt_ref[0]
    tok = page_rows // SUBLANES

    def scores(kf, bias):
        s = jnp.dot(kf.astype(jnp.bfloat16), qt, preferred_element_type=jnp.float32)
        return s * SCALE + bias

    def colmax(s):
        return jnp.max(s.reshape(-1, SUBLANES, LANES), axis=0)

    def probs(s, m):
        n = s.shape[0] // SUBLANES
        p = jnp.exp(s.reshape(n, SUBLANES, LANES) - m[None])
        return jnp.sum(p, axis=0), p.reshape(s.shape).astype(jnp.bfloat16)

    def pv(p, vf):
        return lax.dot_general(p, vf.astype(jnp.bfloat16), (((0,), (0,)), ((), ())),
                               preferred_element_type=jnp.float32)

    ss = [scores(k_refs[g][0], bias_ref[pl.ds(g * page_rows, page_rows), :]) for g in range(n_group)]
    m = jnp.full((SUBLANES, LANES), M_FLOOR, jnp.float32)
    for s in ss:
        m = jnp.maximum(m, colmax(s))
    l = jnp.zeros((SUBLANES, LANES), jnp.float32)
    o = jnp.zeros((LANES, V_DIM), jnp.float32)
    for g in range(n_group):
        lg, p = probs(ss[g], m)
        l = l + lg
        o = o + pv(p, v_refs[g][0])
    m_s[step] = m
    l_s[step] = l
    o_s[step] = o

    @pl.when(step == n_steps - 1)
    def _():
        sn = scores(kn_ref[0], biasn_ref[...])
        mn = jnp.maximum(colmax(sn), M_FLOOR)
        ln, pn = probs(sn, mn)
        m_s[n_steps] = mn
        l_s[n_steps] = ln
        o_s[n_steps] = pv(pn, vn_ref[0])

        row = lax.broadcasted_iota(jnp.int32, (SUBLANES, LANES), 0)
        col = lax.broadcasted_iota(jnp.int32, (SUBLANES, LANES), 1)
        diag = row == col % SUBLANES
        coff = coff_ref[...]
        n_ent = n_steps + 1
        pages_before = [e * n_group for e in range(n_steps)] + [n_steps * n_group]
        mt = [m_s[e] + float(pages_before[e]) * coff for e in range(n_ent)]
        mx = mt[0]
        for e in range(1, n_ent):
            mx = jnp.maximum(mx, mt[e])
        wts = [jnp.exp(mt[e] - mx) for e in range(n_ent)]
        lsum = wts[0] * l_s[0]
        for e in range(1, n_ent):
            lsum = lsum + wts[e] * l_s[e]
        inv = 1.0 / _diag_rows(lsum, diag)
        wrows = [_diag_rows(wts[e], diag) * inv for e in range(n_ent)]
        wmat = jnp.concatenate(wrows + [jnp.zeros((LANES - n_ent, LANES), jnp.float32)], axis=0)
        wt = wmat.T
        out = wt[:, 0:1] * o_s[0]
        for e in range(1, n_ent):
            out = out + wt[:, e:e + 1] * o_s[e]
        half = LANES // 2
        lam = _lambda_full(lq1_ref[...], lk1_ref[...], lq2_ref[...], lk2_ref[...], lam_init)
        od = out[:half] - lam * out[half:]
        o_ref[0] = _subln(od, sg_ref[...], lam_init).astype(o_ref.dtype)


def _sample_attention(q, k_new, v_new, cache_k, cache_v, page_table, slopes, lam_vecs, subln_g,
                      lam_init, n_group):
    db, t, nh, _ = q.shape
    n_phys, page, _, _ = cache_k.shape
    n_pages = page_table.shape[1]
    assert nh == SUBLANES and 2 * t * nh == LANES and n_pages % n_group == 0
    page_rows = page * nh
    n_steps = n_pages // n_group
    ck = cache_k.reshape(n_phys, page_rows, V_DIM)
    cv = cache_v.reshape(n_phys, page_rows, V_DIM)
    q5 = q.reshape(db, t, nh, 2, HEAD_DIM)
    qt = jnp.einsum('bthcd,ce->bcdeth', q5, jnp.eye(2, dtype=q.dtype)).reshape(db, 2 * HEAD_DIM, LANES)
    kn = k_new.reshape(db, t * nh, V_DIM)
    vn = v_new.reshape(db, t * nh, V_DIM)
    col = jnp.arange(LANES)
    col_h, col_q = col % nh, (col // nh) % t
    rows = jnp.arange(n_group * page_rows)
    row_h, row_t = rows % nh, rows // nh
    bias = jnp.where(row_h[:, None] == col_h[None, :], (slopes[row_h] * row_t)[:, None], NEG).astype(jnp.float32)
    rn = jnp.arange(t * nh)
    rn_h, rn_t = rn % nh, rn // nh
    ok = (rn_h[:, None] == col_h[None, :]) & (rn_t[:, None] <= col_q[None, :])
    biasn = jnp.where(ok, (slopes[rn_h] * rn_t)[:, None], NEG).astype(jnp.float32)
    coff = jnp.broadcast_to((slopes[col_h] * page)[None, :], (SUBLANES, LANES)).astype(jnp.float32)

    page_spec = lambda g: pl.BlockSpec((1, page_rows, V_DIM), lambda b, s, pt, g=g: (pt[b, s * n_group + g], 0, 0))
    const = lambda shape: pl.BlockSpec(shape, lambda b, s, pt: tuple(0 for _ in shape))
    per_b = lambda shape: pl.BlockSpec(shape, lambda b, s, pt: (b,) + tuple(0 for _ in shape[1:]))
    grid_spec = pltpu.PrefetchScalarGridSpec(
        num_scalar_prefetch=1,
        grid=(db, n_steps),
        in_specs=([page_spec(g) for g in range(n_group)] + [page_spec(g) for g in range(n_group)]
                  + [per_b((1, 2 * HEAD_DIM, LANES)), per_b((1, t * nh, V_DIM)), per_b((1, t * nh, V_DIM)),
                     const((n_group * page_rows, LANES)), const((t * nh, LANES)), const((SUBLANES, LANES)),
                     const((1, HEAD_DIM)), const((1, HEAD_DIM)), const((1, HEAD_DIM)), const((1, HEAD_DIM)),
                     const((1, V_DIM))]),
        out_specs=per_b((1, t * nh, V_DIM)),
        scratch_shapes=[pltpu.VMEM((n_steps + 1, SUBLANES, LANES), jnp.float32),
                        pltpu.VMEM((n_steps + 1, SUBLANES, LANES), jnp.float32),
                        pltpu.VMEM((n_steps + 1, LANES, V_DIM), jnp.float32)],
    )
    kern = functools.partial(_sample_attn_kernel, n_group=n_group, n_steps=n_steps,
                             page_rows=page_rows, lam_init=lam_init)
    out = pl.pallas_call(
        kern,
        grid_spec=grid_spec,
        out_shape=jax.ShapeDtypeStruct((db, t * nh, V_DIM), jnp.bfloat16),
        compiler_params=_params("parallel", "arbitrary"),
    )(page_table, *([ck] * n_group), *([cv] * n_group), qt.astype(jnp.bfloat16), kn, vn,
      bias, biasn, coff, *lam_vecs, subln_g.reshape(1, V_DIM))
    return out.reshape(db * t, nh * V_DIM)


def _out_proj_kernel(att_ref, cv_ref, wa_ref, wc_ref, x_ref, o_ref):
    acc = jnp.dot(att_ref[...], wa_ref[...], preferred_element_type=jnp.float32)
    acc = acc + jnp.dot(cv_ref[...], wc_ref[...], preferred_element_type=jnp.float32)
    o_ref[...] = x_ref[...] + acc


def _out_proj(att, cv, w_out, x, tm, tn):
    m, half = att.shape
    d = w_out.shape[1]
    return pl.pallas_call(
        _out_proj_kernel,
        grid=(m // tm, d // tn),
        in_specs=[pl.BlockSpec((tm, half), lambda i, j: (i, 0)),
                  pl.BlockSpec((tm, half), lambda i, j: (i, 0)),
                  pl.BlockSpec((half, tn), lambda i, j: (0, j)),
                  pl.BlockSpec((half, tn), lambda i, j: (1, j)),
                  pl.BlockSpec((tm, tn), lambda i, j: (i, j))],
        out_specs=pl.BlockSpec((tm, tn), lambda i, j: (i, j)),
        out_shape=jax.ShapeDtypeStruct((m, d), jnp.float32),
        compiler_params=_params("parallel", "arbitrary"),
    )(att, cv, w_out, w_out, x)


def _up_kernel(h_ref, w_ref, o_ref):
    acc = jnp.dot(h_ref[...], w_ref[...], preferred_element_type=jnp.float32)
    r = jnp.maximum(acc, 0.0)
    o_ref[...] = (r * r).astype(o_ref.dtype)


def _up_proj(h, w_up, tm, tn):
    m, k = h.shape
    n = w_up.shape[1]
    return pl.pallas_call(
        _up_kernel,
        grid=(m // tm, n // tn),
        in_specs=[pl.BlockSpec((tm, k), lambda i, j: (i, 0)),
                  pl.BlockSpec((k, tn), lambda i, j: (0, j))],
        out_specs=pl.BlockSpec((tm, tn), lambda i, j: (i, j)),
        out_shape=jax.ShapeDtypeStruct((m, n), jnp.bfloat16),
        compiler_params=_params("parallel", "arbitrary"),
    )(h, w_up)


def _down_kernel(a_ref, w_ref, x_ref, o_ref, acc_s):
    kk = pl.program_id(2)

    @pl.when(kk == 0)
    def _():
        acc_s[...] = x_ref[...]

    acc_s[...] += jnp.dot(a_ref[...], w_ref[...], preferred_element_type=jnp.float32)

    @pl.when(kk == pl.num_programs(2) - 1)
    def _():
        o_ref[...] = acc_s[...]


def _down_proj(a, w_down, x, tm, tn, tk):
    m, f = a.shape
    d = w_down.shape[1]
    return pl.pallas_call(
        _down_kernel,
        grid=(m // tm, d // tn, f // tk),
        in_specs=[pl.BlockSpec((tm, tk), lambda i, j, k: (i, k)),
                  pl.BlockSpec((tk, tn), lambda i, j, k: (k, j)),
                  pl.BlockSpec((tm, tn), lambda i, j, k: (i, j))],
        out_specs=pl.BlockSpec((tm, tn), lambda i, j, k: (i, j)),
        out_shape=jax.ShapeDtypeStruct((m, d), jnp.float32),
        scratch_shapes=[pltpu.VMEM((tm, tn), jnp.float32)],
        compiler_params=_params("parallel", "parallel", "arbitrary"),
    )(a, w_down, x)


def _tile(n, pref):
    t = min(n, pref)
    assert n % t == 0
    return t


def _block(x, w, layer, attend, conv_halo, zero_first, conv_tt):
    bsz, t, d = x.shape
    m = bsz * t
    att_w = d // 2
    qk_w = att_w
    conv_ch = d - att_w
    nh = att_w // V_DIM
    lam_init = 0.8 - 0.6 * math.exp(-0.3 * layer)
    tm = _tile(m, 1024)
    tn = 1024
    x2 = x.reshape(m, d)

    h = _rms_cast(x2, w['attn_norm_g'], _tile(m, 256))
    gq = jnp.tile(w['q_norm_g'], qk_w // HEAD_DIM).reshape(1, qk_w)
    gk = jnp.tile(w['k_norm_g'], qk_w // HEAD_DIM).reshape(1, qk_w)
    (q_b,) = _proj(_proj_norm_kernel, h, w['w_in'], [0], qk_w, [jnp.bfloat16], tm, tn, gain=gq)
    k_f, k_b = _proj(_proj_norm_kernel, h, w['w_in'], [qk_w], qk_w, [jnp.float32, jnp.bfloat16], tm, tn, gain=gk)
    v_f, v_b = _proj(_proj_plain_kernel, h, w['w_in'], [2 * qk_w], att_w, [jnp.float32, jnp.bfloat16], tm, tn)
    (u,) = _proj(_proj_glu_kernel, h, w['w_in'], [2 * qk_w + att_w, 2 * qk_w + att_w + conv_ch], conv_ch,
                 [jnp.float32], tm, tn // 2)
    u3 = u.reshape(bsz, t, conv_ch)

    att = attend(q_b, k_f, k_b, v_f, v_b, lam_init)
    halo, hb = conv_halo(u3)
    cv = _conv_module(u3, halo, hb, zero_first, w['conv_w'], w['conv_b'], w['conv_ln_g'], w['conv_ln_b'], conv_tt)
    x1 = _out_proj(att, cv.reshape(m, conv_ch), w['w_out'], x2, tm, tn)
    h2 = _rms_cast(x1, w['mlp_norm_g'], _tile(m, 256))
    up = _up_proj(h2, w['w_up'], tm, tn)
    y = _down_proj(up, w['w_down'], x1, tm, tn, _tile(w['w_down'].shape[0], 2048))
    return (y.reshape(bsz, t, d), k_f.reshape(bsz, t, nh, V_DIM), v_f.reshape(bsz, t, nh, V_DIM), u3)


def kernel(x_prompt, x_sample, cache_k, cache_v, state_conv, page_table, attn_norm_g, w_in, q_norm_g, k_norm_g, lambda_q1, lambda_k1, lambda_q2, lambda_k2, subln_g, conv_w, conv_b, conv_ln_g, conv_ln_b, w_out, mlp_norm_g, w_up, w_down):
    depth = w_in.shape[0]
    d = x_prompt.shape[-1]
    nh = (d // 2) // V_DIM
    hist = CONV_WIDTH - 1
    slopes = jnp.exp2(-8.0 * jnp.arange(1, nh + 1, dtype=jnp.float32) / nh)
    xp, xs = x_prompt, x_sample
    outs = [[] for _ in range(6)]
    for l in range(depth):
        bf = lambda a: a[l].astype(jnp.bfloat16)
        w = {'attn_norm_g': attn_norm_g[l], 'w_in': bf(w_in), 'q_norm_g': q_norm_g[l], 'k_norm_g': k_norm_g[l],
             'conv_w': conv_w[l], 'conv_b': conv_b[l], 'conv_ln_g': conv_ln_g[l], 'conv_ln_b': conv_ln_b[l],
             'w_out': bf(w_out), 'mlp_norm_g': mlp_norm_g[l], 'w_up': bf(w_up), 'w_down': bf(w_down)}
        lam_vecs = [a[l].reshape(1, HEAD_DIM) for a in (lambda_q1, lambda_k1, lambda_q2, lambda_k2)]
        sg = subln_g[l]

        bsz, s, _ = xp.shape
        tq = _tile(s, 512)

        def attend_p(q_b, k_f, k_b, v_f, v_b, lam_init):
            r3 = lambda a: a.reshape(bsz, s, nh * V_DIM)
            o = _prompt_attention(r3(q_b), r3(k_b), r3(v_b), slopes, lam_vecs, sg, lam_init, tq)
            return o.reshape(bsz * s, nh * V_DIM)

        conv_tt = _tile(s, 128)
        xp, kp, vp, up_ = _block(xp, w, l, attend_p, lambda u3: (u3, conv_tt // HALO), True, conv_tt)
        cp = up_[:, -hist:]

        db, t, _ = xs.shape
        ck, cvv, st = cache_k[l], cache_v[l], state_conv[l]

        def attend_s(q_b, k_f, k_b, v_f, v_b, lam_init):
            r4 = lambda a: a.reshape(db, t, nh, V_DIM)
            return _sample_attention(r4(q_b), r4(k_f), r4(v_f), ck, cvv, page_table, slopes, lam_vecs, sg,
                                     lam_init, n_group=4)

        halo_s = jnp.pad(st, ((0, 0), (HALO - hist, 0), (0, 0)))
        xs, ksn, vsn, us = _block(xs, w, l, attend_s, lambda u3: (halo_s, 0), False, t)
        cs = jnp.concatenate([st, us], axis=1)[:, -hist:]

        for lst, val in zip(outs, (kp, vp, cp, ksn, vsn, cs)):
            lst.append(val)
    return (xp, xs) + tuple(jnp.stack(o) for o in outs)
```

```python
import functools
import math

import jax
import jax.numpy as jnp
from jax import lax
from jax.experimental import pallas as pl
from jax.experimental.pallas import tpu as pltpu

HEAD_DIM = 128
V_DIM = 2 * HEAD_DIM
CONV_WIDTH = 31
EPS = 1e-6
SCALE = HEAD_DIM ** -0.5
LOG2E = math.log2(math.e)
NEG = -1e30
M_FLOOR = -1e20
LANES = 128
SUBLANES = 8
VMEM_LIMIT = 56 * 1024 * 1024
HALO = 32
BF16 = jnp.bfloat16
F32 = jnp.float32


def _params(*sem):
    return pltpu.CompilerParams(dimension_semantics=sem, vmem_limit_bytes=VMEM_LIMIT)


def _rms_cast_kernel(x_ref, g_ref, o_ref):
    x = x_ref[...]
    ms = jnp.mean(x * x, axis=-1, keepdims=True)
    o_ref[...] = ((x * lax.rsqrt(ms + EPS)) * g_ref[...]).astype(o_ref.dtype)


def _rms_cast(x, g, tm):
    m, d = x.shape
    return pl.pallas_call(
        _rms_cast_kernel,
        grid=(m // tm,),
        in_specs=[pl.BlockSpec((tm, d), lambda i: (i, 0)),
                  pl.BlockSpec((1, d), lambda i: (0, 0))],
        out_specs=pl.BlockSpec((tm, d), lambda i: (i, 0)),
        out_shape=jax.ShapeDtypeStruct((m, d), BF16),
        compiler_params=_params("parallel"),
    )(x, g.reshape(1, d))


def _group_rms(acc, g):
    outs = []
    for c in range(acc.shape[1] // HEAD_DIM):
        xg = acc[:, c * HEAD_DIM:(c + 1) * HEAD_DIM]
        ms = jnp.mean(xg * xg, axis=-1, keepdims=True)
        outs.append(xg * lax.rsqrt(ms + EPS))
    return jnp.concatenate(outs, axis=-1) * g


def _proj_kernel(*refs, n_w, n_out, epilogue, cast_w):
    h_ref, w_refs = refs[0], refs[1:1 + n_w]
    pos = 1 + n_w
    g_ref = None
    if epilogue == "norm":
        g_ref, pos = refs[pos], pos + 1
    o_refs = refs[pos:pos + n_out]
    wb_refs = refs[pos + n_out:]
    h = h_ref[...]
    accs = []
    for i, w_ref in enumerate(w_refs):
        w = w_ref[...]
        if cast_w:
            w = w.astype(BF16)
            wb_refs[i][...] = w
        accs.append(jnp.dot(h, w, preferred_element_type=F32))
    if epilogue == "norm":
        y = _group_rms(accs[0], g_ref[...])
    elif epilogue == "glu":
        y = accs[0] * jax.nn.sigmoid(accs[1])
    elif epilogue == "relu2":
        r = jnp.maximum(accs[0], 0.0)
        y = r * r
    else:
        y = accs[0]
    for o_ref in o_refs:
        o_ref[...] = y.astype(o_ref.dtype)


def _proj(h, ws, n_cols, epilogue, out_dtypes, tm, tn, gain=None, cast_w=False):
    m, k = h.shape
    in_specs = [pl.BlockSpec((tm, k), lambda i, j: (i, 0))]
    args = [h]
    for arr, off in ws:
        in_specs.append(pl.BlockSpec((k, tn), lambda i, j, o=off // tn: (0, o + j)))
        args.append(arr)
    if gain is not None:
        in_specs.append(pl.BlockSpec((1, tn), lambda i, j: (0, j)))
        args.append(gain)
    out_specs = [pl.BlockSpec((tm, tn), lambda i, j: (i, j)) for _ in out_dtypes]
    out_shape = [jax.ShapeDtypeStruct((m, n_cols), dt) for dt in out_dtypes]
    if cast_w:
        assert m == tm
        out_specs += [pl.BlockSpec((k, tn), lambda i, j: (0, j)) for _ in ws]
        out_shape += [jax.ShapeDtypeStruct((k, n_cols), BF16) for _ in ws]
    kern = functools.partial(_proj_kernel, n_w=len(ws), n_out=len(out_dtypes), epilogue=epilogue, cast_w=cast_w)
    return pl.pallas_call(
        kern, grid=(m // tm, n_cols // tn), in_specs=in_specs, out_specs=out_specs, out_shape=out_shape,
        compiler_params=_params("parallel", "arbitrary"),
    )(*args)


def _conv_kernel(halo_ref, u_ref, w_ref, b_ref, lg_ref, lb_ref, o_ref, ext_s, sh_s, y_s,
                 *, zero_first, cchunk):
    tt, ch = u_ref.shape[1], u_ref.shape[2]
    ext_s[pl.ds(0, HALO), :] = halo_ref[0]
    if zero_first:
        @pl.when(pl.program_id(1) == 0)
        def _():
            ext_s[pl.ds(0, HALO), :] = jnp.zeros((HALO, ch), F32)
    ext_s[pl.ds(HALO, tt), :] = u_ref[0]
    first = HALO - (CONV_WIDTH - 1)
    span = sh_s.shape[1]

    def chunk(c, carry):
        c0 = pl.multiple_of(c * cchunk, cchunk)
        for r in range(1, SUBLANES):
            sh_s[r] = ext_s[pl.ds(r, span), pl.ds(c0, cchunk)]
        acc = jnp.broadcast_to(b_ref[:, pl.ds(c0, cchunk)], (tt, cchunk))
        for tap in range(CONV_WIDTH):
            r, base = (first + tap) % SUBLANES, (first + tap) // SUBLANES * SUBLANES
            src = ext_s[pl.ds(base, tt), pl.ds(c0, cchunk)] if r == 0 else sh_s[r, pl.ds(base, tt), :]
            acc = acc + src * w_ref[pl.ds(tap, 1), pl.ds(c0, cchunk)]
        y_s[:, pl.ds(c0, cchunk)] = acc
        return carry

    lax.fori_loop(0, ch // cchunk, chunk, 0)
    y = y_s[...]
    mu = jnp.mean(y, axis=-1, keepdims=True)
    d = y - mu
    var = jnp.mean(d * d, axis=-1, keepdims=True)
    yn = d * lax.rsqrt(var + EPS) * lg_ref[...] + lb_ref[...]
    o_ref[0] = (yn * jax.nn.sigmoid(yn)).astype(o_ref.dtype)


def _conv_module(u, halo, halo_blocks_per_tile, zero_first, conv_w, conv_b, ln_g, ln_b, tt):
    bsz, t, ch = u.shape
    hb = halo_blocks_per_tile
    cchunk = 2 * LANES
    kern = functools.partial(_conv_kernel, zero_first=zero_first, cchunk=cchunk)
    return pl.pallas_call(
        kern,
        grid=(bsz, t // tt),
        in_specs=[pl.BlockSpec((1, HALO, ch), lambda b, i: (b, jnp.maximum(i * hb - 1, 0), 0)),
                  pl.BlockSpec((1, tt, ch), lambda b, i: (b, i, 0)),
                  pl.BlockSpec((CONV_WIDTH, ch), lambda b, i: (0, 0)),
                  pl.BlockSpec((1, ch), lambda b, i: (0, 0)),
                  pl.BlockSpec((1, ch), lambda b, i: (0, 0)),
                  pl.BlockSpec((1, ch), lambda b, i: (0, 0))],
        out_specs=pl.BlockSpec((1, tt, ch), lambda b, i: (b, i, 0)),
        out_shape=jax.ShapeDtypeStruct((bsz, t, ch), BF16),
        scratch_shapes=[pltpu.VMEM((HALO + tt, ch), F32),
                        pltpu.VMEM((SUBLANES, HALO - SUBLANES + tt, cchunk), F32),
                        pltpu.VMEM((tt, ch), F32)],
        compiler_params=_params("parallel", "arbitrary"),
    )(halo, u, conv_w, conv_b.reshape(1, ch), ln_g.reshape(1, ch), ln_b.reshape(1, ch))


def _lambda_full(lq1, lk1, lq2, lk2, lam_init):
    return (jnp.exp(jnp.sum(lq1 * lk1, axis=-1, keepdims=True))
            - jnp.exp(jnp.sum(lq2 * lk2, axis=-1, keepdims=True)) + lam_init)


def _subln(o, g, lam_init):
    ms = jnp.mean(o * o, axis=-1, keepdims=True)
    return (o * lax.rsqrt(ms + EPS)) * g * (1.0 - lam_init)


def _prompt_attn_kernel(slope_ref, q_ref, k_ref, v_ref, bias_ref,
                        lq1_ref, lk1_ref, lq2_ref, lk2_ref, sg_ref, o_ref,
                        vt_s, m_s, l_s, a_s, acc_s, p_s, *, lam_init):
    h = pl.program_id(1)
    qi = pl.program_id(2)
    tq = q_ref.shape[1]
    tile_off = slope_ref[h] * (LOG2E * tq)

    @pl.when(qi == 0)
    def _():
        for j in range(vt_s.shape[0]):
            vt_s[j] = v_ref[0, pl.ds(j * tq, tq), :].astype(F32).T.astype(vt_s.dtype)

    q = q_ref[0]
    m_s[...] = jnp.full(m_s.shape, NEG, F32)
    l_s[...] = jnp.zeros(l_s.shape, F32)
    acc_s[...] = jnp.zeros(acc_s.shape, F32)

    def softmax_tile(j):
        r0 = pl.multiple_of(j * tq, tq)
        k = k_ref[0, pl.ds(r0, tq), :]
        bias = bias_ref[0, jnp.where(j == qi, 1, 0)]
        offs = -tile_off * (qi - j).astype(F32)
        ps, alphas = [], []
        for c in range(2):
            cols = pl.ds(c * tq, tq)
            t = lax.dot_general(k[:, c * HEAD_DIM:(c + 1) * HEAD_DIM], q[:, c * HEAD_DIM:(c + 1) * HEAD_DIM],
                                (((1,), (1,)), ((), ())), preferred_element_type=F32) * (SCALE * LOG2E) + bias
            m_prev = m_s[:, cols]
            m_new = jnp.maximum(m_prev, jnp.max(t, axis=0, keepdims=True) + offs)
            p = jnp.exp2(t - (m_new - offs))
            alpha = jnp.exp2(m_prev - m_new)
            l_s[:, cols] = alpha * l_s[:, cols] + jnp.sum(p, axis=0, keepdims=True)
            m_s[:, cols] = m_new
            ps.append(p.astype(p_s.dtype))
            alphas.append(alpha)
        return ps, alphas

    def stage(ps, alphas):
        for c in range(2):
            p_s[:, pl.ds(c * tq, tq)] = ps[c]
            a_s[:, pl.ds(c * tq, tq)] = alphas[c]

    def accumulate(j):
        acc_s[...] = a_s[...] * acc_s[...] + jnp.dot(vt_s[j], p_s[...], preferred_element_type=F32)

    stage(*softmax_tile(0))

    def body(j, carry):
        ps, alphas = softmax_tile(j)
        accumulate(j - 1)
        stage(ps, alphas)
        return carry

    lax.fori_loop(1, qi + 1, body, 0)
    accumulate(qi)

    lam = _lambda_full(lq1_ref[...], lk1_ref[...], lq2_ref[...], lk2_ref[...], lam_init)
    inv = 1.0 / l_s[...]
    ot = acc_s[:, pl.ds(0, tq)] * inv[:, :tq] - lam * (acc_s[:, pl.ds(tq, tq)] * inv[:, tq:])
    o_ref[0] = _subln(ot.T, sg_ref[...], lam_init).astype(o_ref.dtype)


def _prompt_attention(q, k, v, slopes, lam_vecs, subln_g, lam_init, tq):
    bsz, s, width = q.shape
    nh = width // V_DIM
    rel = (jnp.arange(tq, dtype=jnp.int32)[None, :] - jnp.arange(tq, dtype=jnp.int32)[:, None])
    b_open = (-LOG2E) * slopes[:, None, None] * rel.astype(F32)[None]
    bias = jnp.stack([b_open, jnp.where(rel[None] >= 0, b_open, NEG)], axis=1)
    vec = lambda n: pl.BlockSpec((1, n), lambda b, h, i, sl: (0, 0))
    grid_spec = pltpu.PrefetchScalarGridSpec(
        num_scalar_prefetch=1,
        grid=(bsz, nh, s // tq),
        in_specs=[pl.BlockSpec((1, tq, V_DIM), lambda b, h, i, sl: (b, i, h)),
                  pl.BlockSpec((1, s, V_DIM), lambda b, h, i, sl: (b, 0, h)),
                  pl.BlockSpec((1, s, V_DIM), lambda b, h, i, sl: (b, 0, h)),
                  pl.BlockSpec((1, 2, tq, tq), lambda b, h, i, sl: (h, 0, 0, 0)),
                  vec(HEAD_DIM), vec(HEAD_DIM), vec(HEAD_DIM), vec(HEAD_DIM), vec(V_DIM)],
        out_specs=pl.BlockSpec((1, tq, V_DIM), lambda b, h, i, sl: (b, i, h)),
        scratch_shapes=[pltpu.VMEM((s // tq, V_DIM, tq), BF16),
                        pltpu.VMEM((1, 2 * tq), F32),
                        pltpu.VMEM((1, 2 * tq), F32),
                        pltpu.VMEM((1, 2 * tq), F32),
                        pltpu.VMEM((V_DIM, 2 * tq), F32),
                        pltpu.VMEM((tq, 2 * tq), BF16)],
    )
    return pl.pallas_call(
        functools.partial(_prompt_attn_kernel, lam_init=lam_init),
        grid_spec=grid_spec,
        out_shape=jax.ShapeDtypeStruct((bsz, s, width), BF16),
        compiler_params=_params("parallel", "parallel", "arbitrary"),
    )(slopes, q, k, v, bias, *lam_vecs, subln_g.reshape(1, V_DIM))


def _diag_rows(x, diag):
    return jnp.sum(jnp.where(diag, x, 0.0), axis=0, keepdims=True)


def _sample_attn_kernel(pt_ref, *refs, n_group, n_steps, page_rows, lam_init):
    k_refs = refs[:n_group]
    v_refs = refs[n_group:2 * n_group]
    (qt_ref, kn_ref, vn_ref, bias_ref, biasn_ref, coff_ref,
     lq1_ref, lk1_ref, lq2_ref, lk2_ref, sg_ref, o_ref, m_s, l_s, o_s) = refs[2 * n_group:]
    step = pl.program_id(1)
    qt = qt_ref[0]

    def scores(kf, bias):
        s = jnp.dot(kf.astype(BF16), qt, preferred_element_type=F32)
        return s * (SCALE * LOG2E) + bias

    def colmax(s):
        return jnp.max(s.reshape(-1, SUBLANES, LANES), axis=0)

    def probs(s, m):
        n = s.shape[0] // SUBLANES
        p = jnp.exp2(s.reshape(n, SUBLANES, LANES) - m[None])
        return jnp.sum(p, axis=0), p.reshape(s.shape).astype(BF16)

    def pv(p, vf):
        return lax.dot_general(p, vf.astype(BF16), (((0,), (0,)), ((), ())),
                               preferred_element_type=F32)

    ss = [scores(k_refs[g][0], bias_ref[pl.ds(g * page_rows, page_rows), :]) for g in range(n_group)]
    m = jnp.full((SUBLANES, LANES), M_FLOOR, F32)
    for s in ss:
        m = jnp.maximum(m, colmax(s))
    l = jnp.zeros((SUBLANES, LANES), F32)
    o = jnp.zeros((LANES, V_DIM), F32)
    for g in range(n_group):
        lg, p = probs(ss[g], m)
        l = l + lg
        o = o + pv(p, v_refs[g][0])
    m_s[step] = m
    l_s[step] = l
    o_s[step] = o

    @pl.when(step == n_steps - 1)
    def _():
        sn = scores(kn_ref[0], biasn_ref[...])
        mn = jnp.maximum(colmax(sn), M_FLOOR)
        ln, pn = probs(sn, mn)
        m_s[n_steps] = mn
        l_s[n_steps] = ln
        o_s[n_steps] = pv(pn, vn_ref[0])

        row = lax.broadcasted_iota(jnp.int32, (SUBLANES, LANES), 0)
        col = lax.broadcasted_iota(jnp.int32, (SUBLANES, LANES), 1)
        diag = row == col % SUBLANES
        coff = coff_ref[...]
        n_ent = n_steps + 1
        pages_before = [e * n_group for e in range(n_steps)] + [n_steps * n_group]
        mt = [m_s[e] + float(pages_before[e]) * coff for e in range(n_ent)]
        mx = mt[0]
        for e in range(1, n_ent):
            mx = jnp.maximum(mx, mt[e])
        wts = [jnp.exp2(mt[e] - mx) for e in range(n_ent)]
        lsum = wts[0] * l_s[0]
        for e in range(1, n_ent):
            lsum = lsum + wts[e] * l_s[e]
        inv = 1.0 / _diag_rows(lsum, diag)
        wrows = [_diag_rows(wts[e], diag) * inv for e in range(n_ent)]
        wmat = jnp.concatenate(wrows + [jnp.zeros((LANES - n_ent, LANES), F32)], axis=0)
        wt = wmat.T
        out = wt[:, 0:1] * o_s[0]
        for e in range(1, n_ent):
            out = out + wt[:, e:e + 1] * o_s[e]
        half = LANES // 2
        lam = _lambda_full(lq1_ref[...], lk1_ref[...], lq2_ref[...], lk2_ref[...], lam_init)
        od = out[:half] - lam * out[half:]
        o_ref[0] = _subln(od, sg_ref[...], lam_init).astype(o_ref.dtype)


def _sample_attention(q, k_new, v_new, cache_k, cache_v, page_table, slopes, lam_vecs, subln_g,
                      lam_init, n_group):
    db, t, nh, _ = q.shape
    n_phys, page, _, _ = cache_k.shape
    n_pages = page_table.shape[1]
    assert nh == SUBLANES and 2 * t * nh == LANES and n_pages % n_group == 0
    page_rows = page * nh
    n_steps = n_pages // n_group
    ck = cache_k.reshape(n_phys, page_rows, V_DIM)
    cv = cache_v.reshape(n_phys, page_rows, V_DIM)
    q5 = q.reshape(db, t, nh, 2, HEAD_DIM)
    qt = jnp.einsum('bthcd,ce->bcdeth', q5, jnp.eye(2, dtype=q.dtype)).reshape(db, 2 * HEAD_DIM, LANES)
    kn = k_new.reshape(db, t * nh, V_DIM)
    vn = v_new.reshape(db, t * nh, V_DIM)
    sl2 = slopes * LOG2E
    col = jnp.arange(LANES)
    col_h, col_q = col % nh, (col // nh) % t
    rows = jnp.arange(n_group * page_rows)
    row_h, row_t = rows % nh, rows // nh
    bias = jnp.where(row_h[:, None] == col_h[None, :], (sl2[row_h] * row_t)[:, None], NEG).astype(F32)
    rn = jnp.arange(t * nh)
    rn_h, rn_t = rn % nh, rn // nh
    ok = (rn_h[:, None] == col_h[None, :]) & (rn_t[:, None] <= col_q[None, :])
    biasn = jnp.where(ok, (sl2[rn_h] * rn_t)[:, None], NEG).astype(F32)
    coff = jnp.broadcast_to((sl2[col_h] * page)[None, :], (SUBLANES, LANES)).astype(F32)

    page_spec = lambda g: pl.BlockSpec((1, page_rows, V_DIM), lambda b, s, pt, g=g: (pt[b, s * n_group + g], 0, 0))
    const = lambda shape: pl.BlockSpec(shape, lambda b, s, pt: tuple(0 for _ in shape))
    per_b = lambda shape: pl.BlockSpec(shape, lambda b, s, pt: (b,) + tuple(0 for _ in shape[1:]))
    grid_spec = pltpu.PrefetchScalarGridSpec(
        num_scalar_prefetch=1,
        grid=(db, n_steps),
        in_specs=([page_spec(g) for g in range(n_group)] + [page_spec(g) for g in range(n_group)]
                  + [per_b((1, 2 * HEAD_DIM, LANES)), per_b((1, t * nh, V_DIM)), per_b((1, t * nh, V_DIM)),
                     const((n_group * page_rows, LANES)), const((t * nh, LANES)), const((SUBLANES, LANES)),
                     const((1, HEAD_DIM)), const((1, HEAD_DIM)), const((1, HEAD_DIM)), const((1, HEAD_DIM)),
                     const((1, V_DIM))]),
        out_specs=per_b((1, t * nh, V_DIM)),
        scratch_shapes=[pltpu.VMEM((n_steps + 1, SUBLANES, LANES), F32),
                        pltpu.VMEM((n_steps + 1, SUBLANES, LANES), F32),
                        pltpu.VMEM((n_steps + 1, LANES, V_DIM), F32)],
    )
    kern = functools.partial(_sample_attn_kernel, n_group=n_group, n_steps=n_steps,
                             page_rows=page_rows, lam_init=lam_init)
    out = pl.pallas_call(
        kern,
        grid_spec=grid_spec,
        out_shape=jax.ShapeDtypeStruct((db, t * nh, V_DIM), BF16),
        compiler_params=_params("parallel", "arbitrary"),
    )(page_table, *([ck] * n_group), *([cv] * n_group), qt.astype(BF16), kn, vn,
      bias, biasn, coff, *lam_vecs, subln_g.reshape(1, V_DIM))
    return out.reshape(db * t, nh * V_DIM)


def _out_proj_kernel(att_ref, cv_ref, wa_ref, wc_ref, x_ref, o_ref, *wb_refs):
    wa, wc = wa_ref[...], wc_ref[...]
    if wb_refs:
        wa, wc = wa.astype(BF16), wc.astype(BF16)
        wb_refs[0][...] = wa
        wb_refs[1][...] = wc
    acc = jnp.dot(att_ref[...], wa, preferred_element_type=F32)
    acc = acc + jnp.dot(cv_ref[...], wc, preferred_element_type=F32)
    o_ref[...] = x_ref[...] + acc


def _out_proj(att, cv, wa, wc, x, tm, tn, cast_w=False):
    m, half = att.shape
    d = x.shape[1]
    out_specs = [pl.BlockSpec((tm, tn), lambda i, j: (i, j))]
    out_shape = [jax.ShapeDtypeStruct((m, d), F32)]
    if cast_w:
        assert m == tm
        out_specs += [pl.BlockSpec((half, tn), lambda i, j: (0, j))] * 2
        out_shape += [jax.ShapeDtypeStruct((half, d), BF16)] * 2
    return pl.pallas_call(
        _out_proj_kernel,
        grid=(m // tm, d // tn),
        in_specs=[pl.BlockSpec((tm, half), lambda i, j: (i, 0)),
                  pl.BlockSpec((tm, half), lambda i, j: (i, 0)),
                  pl.BlockSpec((half, tn), lambda i, j, r=wa[1]: (r, j)),
                  pl.BlockSpec((half, tn), lambda i, j, r=wc[1]: (r, j)),
                  pl.BlockSpec((tm, tn), lambda i, j: (i, j))],
        out_specs=out_specs, out_shape=out_shape,
        compiler_params=_params("parallel", "arbitrary"),
    )(att, cv, wa[0], wc[0], x)


def _down_kernel(a_ref, w_ref, x_ref, o_ref, *rest):
    acc_s = rest[-1]
    kk = pl.program_id(2)

    @pl.when(kk == 0)
    def _():
        acc_s[...] = x_ref[...]

    w = w_ref[...]
    if len(rest) == 2:
        w = w.astype(BF16)
        rest[0][...] = w
    acc_s[...] += jnp.dot(a_ref[...], w, preferred_element_type=F32)

    @pl.when(kk == pl.num_programs(2) - 1)
    def _():
        o_ref[...] = acc_s[...]


def _down_proj(a, w_down, x, tm, tn, tk, cast_w=False):
    m, f = a.shape
    d = w_down.shape[1]
    out_specs = [pl.BlockSpec((tm, tn), lambda i, j, k: (i, j))]
    out_shape = [jax.ShapeDtypeStruct((m, d), F32)]
    if cast_w:
        assert m == tm
        out_specs.append(pl.BlockSpec((tk, tn), lambda i, j, k: (k, j)))
        out_shape.append(jax.ShapeDtypeStruct((f, d), BF16))
    return pl.pallas_call(
        _down_kernel,
        grid=(m // tm, d // tn, f // tk),
        in_specs=[pl.BlockSpec((tm, tk), lambda i, j, k: (i, k)),
                  pl.BlockSpec((tk, tn), lambda i, j, k: (k, j)),
                  pl.BlockSpec((tm, tn), lambda i, j, k: (i, j))],
        out_specs=out_specs, out_shape=out_shape,
        scratch_shapes=[pltpu.VMEM((tm, tn), F32)],
        compiler_params=_params("parallel", "parallel", "arbitrary"),
    )(a, w_down, x)


def _tile(n, pref):
    t = min(n, pref)
    assert n % t == 0
    return t


def _block(x, w, layer, attend, conv_halo, zero_first, conv_tt, cast_w):
    bsz, t, d = x.shape
    m = bsz * t
    att_w = d // 2
    qk_w = att_w
    conv_ch = d - att_w
    nh = att_w // V_DIM
    lam_init = 0.8 - 0.6 * math.exp(-0.3 * layer)
    tm = _tile(m, 1024)
    tn = 512 if cast_w else 1024
    x2 = x.reshape(m, d)
    if cast_w:
        assert m == tm
        offs = (0, qk_w, 2 * qk_w, 2 * qk_w + att_w, 2 * qk_w + att_w + conv_ch)
        wq, wk, wv, wa, wg = [(w['w_in'], o) for o in offs]
        wo_a, wo_c = (w['w_out'], 0), (w['w_out'], 1)
        w_up, w_down = w['w_up'], w['w_down']
    else:
        wq, wk, wv, wa, wg = [(w[n], 0) for n in ('wq', 'wk', 'wv', 'wa', 'wg')]
        wo_a, wo_c = (w['wo_a'], 0), (w['wo_c'], 0)
        w_up, w_down = w['w_up'], w['w_down']
    pieces = {}

    def proj(ws, names, n_cols, epilogue, out_dtypes, tn_, gain=None):
        res = _proj(h, ws, n_cols, epilogue, out_dtypes, tm, tn_, gain=gain, cast_w=cast_w)
        if cast_w:
            for name, arr in zip(names, res[len(out_dtypes):]):
                pieces[name] = arr
        return res[:len(out_dtypes)]

    h = _rms_cast(x2, w['attn_norm_g'], _tile(m, 256))
    gq = jnp.tile(w['q_norm_g'], qk_w // HEAD_DIM).reshape(1, qk_w)
    gk = jnp.tile(w['k_norm_g'], qk_w // HEAD_DIM).reshape(1, qk_w)
    (q_b,) = proj([wq], ['wq'], qk_w, "norm", [BF16], tn, gain=gq)
    k_f, k_b = proj([wk], ['wk'], qk_w, "norm", [F32, BF16], tn, gain=gk)
    v_f, v_b = proj([wv], ['wv'], att_w, "plain", [F32, BF16], tn)
    (u,) = proj([wa, wg], ['wa', 'wg'], conv_ch, "glu", [F32], tn // 2)
    u3 = u.reshape(bsz, t, conv_ch)

    att = attend(q_b, k_f, k_b, v_f, v_b, lam_init)
    halo, hb = conv_halo(u3)
    cv = _conv_module(u3, halo, hb, zero_first, w['conv_w'], w['conv_b'], w['conv_ln_g'], w['conv_ln_b'], conv_tt)
    res = _out_proj(att, cv.reshape(m, conv_ch), wo_a, wo_c, x2, tm, tn, cast_w=cast_w)
    x1 = res[0]
    if cast_w:
        pieces['wo_a'], pieces['wo_c'] = res[1], res[2]
    h = _rms_cast(x1, w['mlp_norm_g'], _tile(m, 256))
    (up,) = proj([(w_up, 0)], ['w_up'], w_up.shape[1], "relu2", [BF16], tn)
    res = _down_proj(up, w_down, x1, tm, tn, _tile(w_down.shape[0], 4096 if cast_w else 2048), cast_w=cast_w)
    if cast_w:
        pieces['w_down'] = res[1]
    y = res[0]
    return (y.reshape(bsz, t, d), k_f.reshape(bsz, t, nh, V_DIM), v_f.reshape(bsz, t, nh, V_DIM), u3), pieces


def kernel(x_prompt, x_sample, cache_k, cache_v, state_conv, page_table, attn_norm_g, w_in, q_norm_g, k_norm_g, lambda_q1, lambda_k1, lambda_q2, lambda_k2, subln_g, conv_w, conv_b, conv_ln_g, conv_ln_b, w_out, mlp_norm_g, w_up, w_down):
    depth = w_in.shape[0]
    d = x_prompt.shape[-1]
    nh = (d // 2) // V_DIM
    hist = CONV_WIDTH - 1
    slopes = jnp.exp2(-8.0 * jnp.arange(1, nh + 1, dtype=F32) / nh)
    xp, xs = x_prompt, x_sample
    outs = [[] for _ in range(6)]
    for l in range(depth):
        small = {'attn_norm_g': attn_norm_g[l], 'q_norm_g': q_norm_g[l], 'k_norm_g': k_norm_g[l],
                 'conv_w': conv_w[l], 'conv_b': conv_b[l], 'conv_ln_g': conv_ln_g[l], 'conv_ln_b': conv_ln_b[l],
                 'mlp_norm_g': mlp_norm_g[l]}
        big = {'w_in': w_in[l], 'w_out': w_out[l], 'w_up': w_up[l], 'w_down': w_down[l]}
        lam_vecs = [a[l].reshape(1, HEAD_DIM) for a in (lambda_q1, lambda_k1, lambda_q2, lambda_k2)]
        sg = subln_g[l]

        db, t, _ = xs.shape
        ck, cvv, st = cache_k[l], cache_v[l], state_conv[l]

        def attend_s(q_b, k_f, k_b, v_f, v_b, lam_init):
            r4 = lambda a: a.reshape(db, t, nh, V_DIM)
            return _sample_attention(r4(q_b), r4(k_f), r4(v_f), ck, cvv, page_table, slopes, lam_vecs, sg,
                                     lam_init, n_group=min(8, page_table.shape[1]))

        halo_s = jnp.pad(st, ((0, 0), (HALO - hist, 0), (0, 0)))
        (xs, ksn, vsn, us), pieces = _block(xs, {**small, **big}, l, attend_s, lambda u3: (halo_s, 0), False, t, True)
        cs = jnp.concatenate([st, us], axis=1)[:, -hist:]

        bsz, s, _ = xp.shape
        tq = _tile(s, 512)

        def attend_p(q_b, k_f, k_b, v_f, v_b, lam_init):
            r3 = lambda a: a.reshape(bsz, s, nh * V_DIM)
            o = _prompt_attention(r3(q_b), r3(k_b), r3(v_b), slopes, lam_vecs, sg, lam_init, tq)
            return o.reshape(bsz * s, nh * V_DIM)

        conv_tt = _tile(s, 128)
        (xp, kp, vp, up_), _ = _block(xp, {**small, **pieces}, l, attend_p, lambda u3: (u3, conv_tt // HALO), True,
                                      conv_tt, False)
        cp = up_[:, -hist:]

        for lst, val in zip(outs, (kp, vp, cp, ksn, vsn, cs)):
            lst.append(val)
    return (xp, xs) + tuple(jnp.stack(o) for o in outs)
```

```python
import functools
import math

import jax
import jax.numpy as jnp
from jax import lax
from jax.experimental import pallas as pl
from jax.experimental.pallas import tpu as pltpu

HEAD_DIM = 128
V_DIM = 2 * HEAD_DIM
CONV_WIDTH = 31
EPS = 1e-6
SCALE = HEAD_DIM ** -0.5
LOG2E = math.log2(math.e)
NEG = -1e30
M_FLOOR = -1e20
LANES = 128
SUBLANES = 8
VMEM_LIMIT = 56 * 1024 * 1024
HALO = 32
PAGES_PER_STEP = 4
CAST_BLOCK_BYTES = 8 * 1024 * 1024
HOST_PIECES = 4
BF16 = jnp.bfloat16
F32 = jnp.float32


def _params(*sem):
    return pltpu.CompilerParams(dimension_semantics=sem, vmem_limit_bytes=VMEM_LIMIT)


def _tile(n, pref):
    t = min(n, pref)
    assert n % t == 0
    return t


def _rms_cast_kernel(x_ref, g_ref, o_ref):
    x = x_ref[...]
    ms = jnp.mean(x * x, axis=-1, keepdims=True)
    o_ref[...] = ((x * lax.rsqrt(ms + EPS)) * g_ref[...]).astype(o_ref.dtype)


def _rms_cast(x, g, tm):
    m, d = x.shape
    return pl.pallas_call(
        _rms_cast_kernel,
        grid=(m // tm,),
        in_specs=[pl.BlockSpec((tm, d), lambda i: (i, 0)),
                  pl.BlockSpec((1, d), lambda i: (0, 0))],
        out_specs=pl.BlockSpec((tm, d), lambda i: (i, 0)),
        out_shape=jax.ShapeDtypeStruct((m, d), BF16),
        compiler_params=_params("parallel"),
    )(x, g.reshape(1, d))


def _cast_kernel(w_ref, o_ref):
    o_ref[...] = w_ref[...].astype(o_ref.dtype)


def _cast_bf16(w):
    r, c = w.shape
    tr = _tile(r, max(SUBLANES, CAST_BLOCK_BYTES // (4 * c)))
    return pl.pallas_call(
        _cast_kernel,
        grid=(r // tr,),
        in_specs=[pl.BlockSpec((tr, c), lambda i: (i, 0))],
        out_specs=pl.BlockSpec((tr, c), lambda i: (i, 0)),
        out_shape=jax.ShapeDtypeStruct((r, c), BF16),
        compiler_params=_params("parallel"),
    )(w)


def _group_rms(acc, g):
    outs = []
    for c in range(acc.shape[1] // HEAD_DIM):
        xg = acc[:, c * HEAD_DIM:(c + 1) * HEAD_DIM]
        ms = jnp.mean(xg * xg, axis=-1, keepdims=True)
        outs.append(xg * lax.rsqrt(ms + EPS))
    return jnp.concatenate(outs, axis=-1) * g


def _relu2(acc):
    r = jnp.maximum(acc, 0.0)
    return r * r


def _proj_kernel(*refs, n_w, n_out, epilogue, cast_w):
    h_ref, w_refs = refs[0], refs[1:1 + n_w]
    pos = 1 + n_w
    g_ref = None
    if epilogue == "norm":
        g_ref, pos = refs[pos], pos + 1
    o_refs = refs[pos:pos + n_out]
    wb_refs = refs[pos + n_out:]
    h = h_ref[...]
    accs = []
    for i, w_ref in enumerate(w_refs):
        w = w_ref[...]
        if cast_w:
            w = w.astype(BF16)
            wb_refs[i][...] = w
        accs.append(jnp.dot(h, w, preferred_element_type=F32))
    if epilogue == "norm":
        y = _group_rms(accs[0], g_ref[...])
    elif epilogue == "glu":
        y = accs[0] * jax.nn.sigmoid(accs[1])
    elif epilogue == "relu2":
        y = _relu2(accs[0])
    else:
        y = accs[0]
    for o_ref in o_refs:
        o_ref[...] = y.astype(o_ref.dtype)


def _proj(h, ws, n_cols, epilogue, out_dtypes, tm, tn, gain=None, cast_w=False):
    m, k = h.shape
    in_specs = [pl.BlockSpec((tm, k), lambda i, j: (i, 0))]
    args = [h]
    for arr, off in ws:
        in_specs.append(pl.BlockSpec((k, tn), lambda i, j, o=off // tn: (0, o + j)))
        args.append(arr)
    if gain is not None:
        in_specs.append(pl.BlockSpec((1, tn), lambda i, j: (0, j)))
        args.append(gain)
    out_specs = [pl.BlockSpec((tm, tn), lambda i, j: (i, j)) for _ in out_dtypes]
    out_shape = [jax.ShapeDtypeStruct((m, n_cols), dt) for dt in out_dtypes]
    if cast_w:
        assert m == tm
        out_specs += [pl.BlockSpec((k, tn), lambda i, j: (0, j)) for _ in ws]
        out_shape += [jax.ShapeDtypeStruct((k, n_cols), BF16) for _ in ws]
    kern = functools.partial(_proj_kernel, n_w=len(ws), n_out=len(out_dtypes), epilogue=epilogue, cast_w=cast_w)
    return pl.pallas_call(
        kern, grid=(m // tm, n_cols // tn), in_specs=in_specs, out_specs=out_specs, out_shape=out_shape,
        compiler_params=_params("parallel", "arbitrary"),
    )(*args)


def _in_proj(h, w, ws, tm, tn, cast_w):
    wq, wk, wv, wa, wg = ws
    qk_w = w['q_gain'].shape[1]
    res_q = _proj(h, [wq], qk_w, "norm", [BF16], tm, tn, gain=w['q_gain'], cast_w=cast_w)
    res_k = _proj(h, [wk], qk_w, "norm", [F32, BF16], tm, tn, gain=w['k_gain'], cast_w=cast_w)
    res_v = _proj(h, [wv], qk_w, "plain", [F32, BF16], tm, tn, cast_w=cast_w)
    res_u = _proj(h, [wa, wg], qk_w, "glu", [F32], tm, tn // 2, cast_w=cast_w)
    acts = (res_q[0], res_k[0], res_k[1], res_v[0], res_v[1], res_u[0])
    pieces = (res_q[1], res_k[2], res_v[2], res_u[1], res_u[2]) if cast_w else None
    return acts, pieces


def _conv_kernel(halo_ref, u_ref, w_ref, b_ref, lg_ref, lb_ref, o_ref, ext_s, sh_s, y_s,
                 *, zero_first, cchunk):
    tt, ch = u_ref.shape[1], u_ref.shape[2]
    ext_s[pl.ds(0, HALO), :] = halo_ref[0]
    if zero_first:
        @pl.when(pl.program_id(1) == 0)
        def _():
            ext_s[pl.ds(0, HALO), :] = jnp.zeros((HALO, ch), F32)
    ext_s[pl.ds(HALO, tt), :] = u_ref[0]
    first = HALO - (CONV_WIDTH - 1)
    span = sh_s.shape[1]

    def chunk(c, carry):
        c0 = pl.multiple_of(c * cchunk, cchunk)
        for r in range(1, SUBLANES):
            sh_s[r] = ext_s[pl.ds(r, span), pl.ds(c0, cchunk)]
        acc = jnp.broadcast_to(b_ref[:, pl.ds(c0, cchunk)], (tt, cchunk))
        for tap in range(CONV_WIDTH):
            r, base = (first + tap) % SUBLANES, (first + tap) // SUBLANES * SUBLANES
            src = ext_s[pl.ds(base, tt), pl.ds(c0, cchunk)] if r == 0 else sh_s[r, pl.ds(base, tt), :]
            acc = acc + src * w_ref[pl.ds(tap, 1), pl.ds(c0, cchunk)]
        y_s[:, pl.ds(c0, cchunk)] = acc
        return carry

    lax.fori_loop(0, ch // cchunk, chunk, 0)
    y = y_s[...]
    mu = jnp.mean(y, axis=-1, keepdims=True)
    d = y - mu
    var = jnp.mean(d * d, axis=-1, keepdims=True)
    yn = d * lax.rsqrt(var + EPS) * lg_ref[...] + lb_ref[...]
    o_ref[0] = (yn * jax.nn.sigmoid(yn)).astype(o_ref.dtype)


def _conv_module(u, halo, halo_blocks_per_tile, zero_first, w, tt):
    bsz, t, ch = u.shape
    hb = halo_blocks_per_tile
    cchunk = 2 * LANES
    kern = functools.partial(_conv_kernel, zero_first=zero_first, cchunk=cchunk)
    row = lambda: pl.BlockSpec((1, ch), lambda b, i: (0, 0))
    return pl.pallas_call(
        kern,
        grid=(bsz, t // tt),
        in_specs=[pl.BlockSpec((1, HALO, ch), lambda b, i: (b, jnp.maximum(i * hb - 1, 0), 0)),
                  pl.BlockSpec((1, tt, ch), lambda b, i: (b, i, 0)),
                  pl.BlockSpec((CONV_WIDTH, ch), lambda b, i: (0, 0)),
                  row(), row(), row()],
        out_specs=pl.BlockSpec((1, tt, ch), lambda b, i: (b, i, 0)),
        out_shape=jax.ShapeDtypeStruct((bsz, t, ch), BF16),
        scratch_shapes=[pltpu.VMEM((HALO + tt, ch), F32),
                        pltpu.VMEM((SUBLANES, HALO - SUBLANES + tt, cchunk), F32),
                        pltpu.VMEM((tt, ch), F32)],
        compiler_params=_params("parallel", "arbitrary"),
    )(halo, u, w['conv_w'], w['conv_b'].reshape(1, ch), w['conv_ln_g'].reshape(1, ch), w['conv_ln_b'].reshape(1, ch))


def _lambda_full(lq1, lk1, lq2, lk2, lam_init):
    return (jnp.exp(jnp.sum(lq1 * lk1, axis=-1, keepdims=True))
            - jnp.exp(jnp.sum(lq2 * lk2, axis=-1, keepdims=True)) + lam_init)


def _subln(o, g, lam_init):
    ms = jnp.mean(o * o, axis=-1, keepdims=True)
    return (o * lax.rsqrt(ms + EPS)) * g * (1.0 - lam_init)


def _prompt_attn_kernel(slope_ref, q_ref, k_ref, v_ref, bias_ref,
                        lq1_ref, lk1_ref, lq2_ref, lk2_ref, sg_ref, o_ref,
                        vt_s, m_s, l_s, a_s, acc_s, p_s, *, lam_init):
    h = pl.program_id(1)
    qi = pl.program_id(2)
    tq = q_ref.shape[1]
    tile_off = slope_ref[h] * (LOG2E * tq)

    @pl.when(qi == 0)
    def _():
        for j in range(vt_s.shape[0]):
            vt_s[j] = v_ref[0, pl.ds(j * tq, tq), :].astype(F32).T.astype(vt_s.dtype)

    q = q_ref[0]
    m_s[...] = jnp.full(m_s.shape, NEG, F32)
    l_s[...] = jnp.zeros(l_s.shape, F32)
    acc_s[...] = jnp.zeros(acc_s.shape, F32)

    def softmax_tile(j):
        r0 = pl.multiple_of(j * tq, tq)
        k = k_ref[0, pl.ds(r0, tq), :]
        bias = bias_ref[0, jnp.where(j == qi, 1, 0)]
        offs = -tile_off * (qi - j).astype(F32)
        ps, alphas = [], []
        for c in range(2):
            cols = pl.ds(c * tq, tq)
            t = lax.dot_general(k[:, c * HEAD_DIM:(c + 1) * HEAD_DIM], q[:, c * HEAD_DIM:(c + 1) * HEAD_DIM],
                                (((1,), (1,)), ((), ())), preferred_element_type=F32) * (SCALE * LOG2E) + bias
            m_prev = m_s[:, cols]
            m_new = jnp.maximum(m_prev, jnp.max(t, axis=0, keepdims=True) + offs)
            p = jnp.exp2(t - (m_new - offs))
            alpha = jnp.exp2(m_prev - m_new)
            l_s[:, cols] = alpha * l_s[:, cols] + jnp.sum(p, axis=0, keepdims=True)
            m_s[:, cols] = m_new
            ps.append(p.astype(p_s.dtype))
            alphas.append(alpha)
        return ps, alphas

    def stage(ps, alphas):
        for c in range(2):
            p_s[:, pl.ds(c * tq, tq)] = ps[c]
            a_s[:, pl.ds(c * tq, tq)] = alphas[c]

    def accumulate(j):
        acc_s[...] = a_s[...] * acc_s[...] + jnp.dot(vt_s[j], p_s[...], preferred_element_type=F32)

    stage(*softmax_tile(0))

    def body(j, carry):
        ps, alphas = softmax_tile(j)
        accumulate(j - 1)
        stage(ps, alphas)
        return carry

    lax.fori_loop(1, qi + 1, body, 0)
    accumulate(qi)

    lam = _lambda_full(lq1_ref[...], lk1_ref[...], lq2_ref[...], lk2_ref[...], lam_init)
    inv = 1.0 / l_s[...]
    ot = acc_s[:, pl.ds(0, tq)] * inv[:, :tq] - lam * (acc_s[:, pl.ds(tq, tq)] * inv[:, tq:])
    o_ref[0] = _subln(ot.T, sg_ref[...], lam_init).astype(o_ref.dtype)


def _prompt_attention(q, k, v, slopes, lam_vecs, subln_g, lam_init, tq):
    bsz, s, width = q.shape
    nh = width // V_DIM
    rel = (jnp.arange(tq, dtype=jnp.int32)[None, :] - jnp.arange(tq, dtype=jnp.int32)[:, None])
    b_open = (-LOG2E) * slopes[:, None, None] * rel.astype(F32)[None]
    bias = jnp.stack([b_open, jnp.where(rel[None] >= 0, b_open, NEG)], axis=1)
    vec = lambda n: pl.BlockSpec((1, n), lambda b, h, i, sl: (0, 0))
    grid_spec = pltpu.PrefetchScalarGridSpec(
        num_scalar_prefetch=1,
        grid=(bsz, nh, s // tq),
        in_specs=[pl.BlockSpec((1, tq, V_DIM), lambda b, h, i, sl: (b, i, h)),
                  pl.BlockSpec((1, s, V_DIM), lambda b, h, i, sl: (b, 0, h)),
                  pl.BlockSpec((1, s, V_DIM), lambda b, h, i, sl: (b, 0, h)),
                  pl.BlockSpec((1, 2, tq, tq), lambda b, h, i, sl: (h, 0, 0, 0)),
                  vec(HEAD_DIM), vec(HEAD_DIM), vec(HEAD_DIM), vec(HEAD_DIM), vec(V_DIM)],
        out_specs=pl.BlockSpec((1, tq, V_DIM), lambda b, h, i, sl: (b, i, h)),
        scratch_shapes=[pltpu.VMEM((s // tq, V_DIM, tq), BF16),
                        pltpu.VMEM((1, 2 * tq), F32),
                        pltpu.VMEM((1, 2 * tq), F32),
                        pltpu.VMEM((1, 2 * tq), F32),
                        pltpu.VMEM((V_DIM, 2 * tq), F32),
                        pltpu.VMEM((tq, 2 * tq), BF16)],
    )
    return pl.pallas_call(
        functools.partial(_prompt_attn_kernel, lam_init=lam_init),
        grid_spec=grid_spec,
        out_shape=jax.ShapeDtypeStruct((bsz, s, width), BF16),
        compiler_params=_params("parallel", "parallel", "arbitrary"),
    )(slopes, q, k, v, bias, *lam_vecs, subln_g.reshape(1, V_DIM))


def _diag_rows(x, diag):
    return jnp.sum(jnp.where(diag, x, 0.0), axis=0, keepdims=True)


def _decode_step(step, refs, host, *, n_group, n_steps, lam_init):
    k_refs = refs[:n_group]
    v_refs = refs[n_group:2 * n_group]
    (qt_ref, kn_ref, vn_ref, bias_ref, biasn_ref, coff_ref,
     lq1_ref, lk1_ref, lq2_ref, lk2_ref, sg_ref, o_ref, m_s, l_s, o_s) = refs[2 * n_group:]
    qt = qt_ref[0]
    coff = coff_ref[...]

    def scores(kf, bias):
        s = jnp.dot(kf.astype(BF16), qt, preferred_element_type=F32)
        return s * (SCALE * LOG2E) + bias

    def colmax(s):
        return jnp.max(s.reshape(-1, SUBLANES, LANES), axis=0)

    def probs(s, m):
        n = s.shape[0] // SUBLANES
        p = jnp.exp2(s.reshape(n, SUBLANES, LANES) - m[None])
        return jnp.sum(p, axis=0), p.reshape(s.shape).astype(BF16)

    def pv(p, vf):
        return lax.dot_general(p, vf.astype(BF16), (((0,), (0,)), ((), ())),
                               preferred_element_type=F32)

    n_slots = 2 * n_group
    slot_of = [i * n_slots // max(len(host), 1) for i in range(len(host))]

    def host_work(slot):
        for i, piece in enumerate(host):
            if slot_of[i] == slot:
                piece()

    bias = bias_ref[...]
    ss = []
    for g in range(n_group):
        host_work(g)
        ss.append(scores(k_refs[g][0], bias))
    m = jnp.full((SUBLANES, LANES), M_FLOOR, F32)
    for g, s in enumerate(ss):
        m = jnp.maximum(m, colmax(s) + float(g) * coff)
    l = jnp.zeros((SUBLANES, LANES), F32)
    o = jnp.zeros((LANES, V_DIM), F32)
    for g in range(n_group):
        host_work(n_group + g)
        lg, p = probs(ss[g], m - float(g) * coff)
        l = l + lg
        o = o + pv(p, v_refs[g][0])
    m_s[step] = m
    l_s[step] = l
    o_s[step] = o

    @pl.when(step == n_steps - 1)
    def _():
        sn = scores(kn_ref[0], biasn_ref[...])
        mn = jnp.maximum(colmax(sn), M_FLOOR)
        ln, pn = probs(sn, mn)
        m_s[n_steps] = mn
        l_s[n_steps] = ln
        o_s[n_steps] = pv(pn, vn_ref[0])

        row = lax.broadcasted_iota(jnp.int32, (SUBLANES, LANES), 0)
        col = lax.broadcasted_iota(jnp.int32, (SUBLANES, LANES), 1)
        diag = row == col % SUBLANES
        n_ent = n_steps + 1
        pages_before = [e * n_group for e in range(n_steps)] + [n_steps * n_group]
        mt = [m_s[e] + float(pages_before[e]) * coff for e in range(n_ent)]
        mx = mt[0]
        for e in range(1, n_ent):
            mx = jnp.maximum(mx, mt[e])
        wts = [jnp.exp2(mt[e] - mx) for e in range(n_ent)]
        lsum = wts[0] * l_s[0]
        for e in range(1, n_ent):
            lsum = lsum + wts[e] * l_s[e]
        inv = 1.0 / _diag_rows(lsum, diag)
        wrows = [_diag_rows(wts[e], diag) * inv for e in range(n_ent)]
        wmat = jnp.concatenate(wrows + [jnp.zeros((LANES - n_ent, LANES), F32)], axis=0)
        wt = wmat.T
        out = wt[:, 0:1] * o_s[0]
        for e in range(1, n_ent):
            out = out + wt[:, e:e + 1] * o_s[e]
        half = LANES // 2
        lam = _lambda_full(lq1_ref[...], lk1_ref[...], lq2_ref[...], lk2_ref[...], lam_init)
        od = out[:half] - lam * out[half:]
        o_ref[0] = _subln(od, sg_ref[...], lam_init).astype(o_ref.dtype)


class _Decode:
    def __init__(self, q, k_new, v_new, cache_k, cache_v, slopes, lam_vecs, subln_g, lam_init, n_pages, b0, nb):
        db, t, nh, _ = q.shape
        n_phys, page, _, _ = cache_k.shape
        self.n_group = n_group = min(PAGES_PER_STEP, n_pages)
        assert nh == SUBLANES and 2 * t * nh == LANES and n_pages % n_group == 0
        self.n_steps = n_steps = n_pages // n_group
        self.n_lin = nb * n_steps
        self.b0, self.nb, self.lam_init = b0, nb, lam_init
        rows_q = t * nh
        page_rows = page * nh
        ck = cache_k.reshape(n_phys, page_rows, V_DIM)
        cv = cache_v.reshape(n_phys, page_rows, V_DIM)
        q5 = q.reshape(db, t, nh, 2, HEAD_DIM)
        qt = jnp.einsum('bthcd,ce->bcdeth', q5, jnp.eye(2, dtype=q.dtype)).reshape(db, 2 * HEAD_DIM, LANES)
        kn = k_new.reshape(db, rows_q, V_DIM)
        vn = v_new.reshape(db, rows_q, V_DIM)
        sl2 = slopes * LOG2E
        col = jnp.arange(LANES)
        col_h, col_q = col % nh, (col // nh) % t
        rows = jnp.arange(page_rows)
        row_h, row_t = rows % nh, rows // nh
        bias = jnp.where(row_h[:, None] == col_h[None, :], (sl2[row_h] * row_t)[:, None], NEG).astype(F32)
        rn = jnp.arange(rows_q)
        rn_h, rn_t = rn % nh, rn // nh
        ok = (rn_h[:, None] == col_h[None, :]) & (rn_t[:, None] <= col_q[None, :])
        biasn = jnp.where(ok, (sl2[rn_h] * rn_t)[:, None], NEG).astype(F32)
        coff = jnp.broadcast_to((sl2[col_h] * page)[None, :], (SUBLANES, LANES)).astype(F32)
        self.args = ([ck] * n_group + [cv] * n_group
                     + [qt.astype(BF16), kn, vn, bias, biasn, coff, *lam_vecs, subln_g.reshape(1, V_DIM)])
        self.block_shapes = ([(1, page_rows, V_DIM)] * (2 * n_group)
                             + [(1, 2 * HEAD_DIM, LANES), (1, rows_q, V_DIM), (1, rows_q, V_DIM),
                                (page_rows, LANES), (rows_q, LANES), (SUBLANES, LANES),
                                (1, HEAD_DIM), (1, HEAD_DIM), (1, HEAD_DIM), (1, HEAD_DIM), (1, V_DIM)])
        self.out_block = (1, rows_q, V_DIM)
        self.out_shape = jax.ShapeDtypeStruct((nb, rows_q, V_DIM), BF16)
        self.scratch = [pltpu.VMEM((n_steps + 1, SUBLANES, LANES), F32),
                        pltpu.VMEM((n_steps + 1, SUBLANES, LANES), F32),
                        pltpu.VMEM((n_steps + 1, LANES, V_DIM), F32)]

    def specs(self, lin_of):
        n_group, n_steps, b0 = self.n_group, self.n_steps, self.b0
        last = self.n_lin - 1

        def seq(*idx):
            lin = jnp.minimum(lin_of(*idx[:-1]), last)
            return lin // n_steps, lin % n_steps

        def page_map(g):
            def f(*idx):
                b, ps = seq(*idx)
                return (idx[-1][b0 + b, ps * n_group + g], 0, 0)
            return f

        def per_b(shape, off):
            def f(*idx):
                return (off + seq(*idx)[0],) + (0,) * (len(shape) - 1)
            return f

        const = lambda shape: (lambda *idx: (0,) * len(shape))
        in_specs = []
        for i, shape in enumerate(self.block_shapes):
            if i < 2 * n_group:
                in_specs.append(pl.BlockSpec(shape, page_map(i % n_group)))
            elif i < 2 * n_group + 3:
                in_specs.append(pl.BlockSpec(shape, per_b(shape, b0)))
            else:
                in_specs.append(pl.BlockSpec(shape, const(shape)))
        out_spec = pl.BlockSpec(self.out_block, per_b(self.out_block, 0))
        return in_specs, out_spec

    def run(self, lin, refs, n_host_steps, host):
        kw = dict(n_group=self.n_group, n_steps=self.n_steps, lam_init=self.lam_init)
        if n_host_steps == self.n_lin:
            _decode_step(lin % self.n_steps, refs, host, **kw)
        else:
            for piece in host:
                piece()

            @pl.when(lin < self.n_lin)
            def _():
                _decode_step(lin % self.n_steps, refs, [], **kw)


def _row_pieces(tm, piece_fn):
    rows = tm // HOST_PIECES
    return [functools.partial(piece_fn, pl.ds(r * rows, rows)) for r in range(HOST_PIECES)]


def _out_proj_kernel(att_ref, cv_ref, wa_ref, wc_ref, x_ref, o_ref):
    acc = jnp.dot(att_ref[...], wa_ref[...], preferred_element_type=F32)
    acc = acc + jnp.dot(cv_ref[...], wc_ref[...], preferred_element_type=F32)
    o_ref[...] = x_ref[...] + acc


def _out_proj(att, cv, w_out, x, tm, tn):
    m, half = att.shape
    d = x.shape[1]
    return pl.pallas_call(
        _out_proj_kernel,
        grid=(m // tm, d // tn),
        in_specs=[pl.BlockSpec((tm, half), lambda i, j: (i, 0)),
                  pl.BlockSpec((tm, half), lambda i, j: (i, 0)),
                  pl.BlockSpec((half, tn), lambda i, j: (0, j)),
                  pl.BlockSpec((half, tn), lambda i, j: (1, j)),
                  pl.BlockSpec((tm, tn), lambda i, j: (i, j))],
        out_specs=pl.BlockSpec((tm, tn), lambda i, j: (i, j)),
        out_shape=jax.ShapeDtypeStruct((m, d), F32),
        compiler_params=_params("parallel", "arbitrary"),
    )(att, cv, w_out, w_out, x)


def _up_kernel(pt_ref, h_ref, w_ref, *rest, dec, n_host_steps):
    n_in = len(dec.block_shapes) if dec is not None else 0
    o_ref = rest[n_in + 1] if dec is not None else rest[0]

    def piece(rows):
        acc = jnp.dot(h_ref[rows, :], w_ref[...], preferred_element_type=F32)
        o_ref[rows, :] = _relu2(acc).astype(o_ref.dtype)

    if dec is None:
        piece(pl.ds(0, o_ref.shape[0]))
    else:
        refs = rest[:n_in] + (rest[n_in],) + rest[n_in + 2:]
        lin = pl.program_id(0) * pl.num_programs(1) + pl.program_id(1)
        dec.run(lin, refs, n_host_steps, _row_pieces(o_ref.shape[0], piece))


def _up_proj(h, w_up, page_table, dec, tm, tn):
    m, k = h.shape
    n = w_up.shape[1]
    grid = (m // tm, n // tn)
    in_specs = [pl.BlockSpec((tm, k), lambda i, j, pt: (i, 0)),
                pl.BlockSpec((k, tn), lambda i, j, pt: (0, j))]
    out_specs = [pl.BlockSpec((tm, tn), lambda i, j, pt: (i, j))]
    out_shape = [jax.ShapeDtypeStruct((m, n), BF16)]
    args, scratch = [h, w_up], []
    if dec is not None:
        assert grid[0] * grid[1] >= dec.n_lin
        d_in, d_out = dec.specs(lambda i, j: i * grid[1] + j)
        in_specs += d_in
        out_specs = [d_out] + out_specs
        out_shape = [dec.out_shape] + out_shape
        args += dec.args
        scratch = dec.scratch
    res = pl.pallas_call(
        functools.partial(_up_kernel, dec=dec, n_host_steps=grid[0] * grid[1]),
        grid_spec=pltpu.PrefetchScalarGridSpec(num_scalar_prefetch=1, grid=grid, in_specs=in_specs,
                                               out_specs=out_specs, scratch_shapes=scratch),
        out_shape=out_shape,
        compiler_params=_params("arbitrary", "arbitrary"),
    )(page_table, *args)
    return (res[1], res[0]) if dec is not None else (res[0], None)


def _down_kernel(pt_ref, a_ref, w_ref, x_ref, *rest, dec, n_host_steps):
    kk = pl.program_id(2)
    n_in = len(dec.block_shapes) if dec is not None else 0
    o_ref = rest[n_in + 1] if dec is not None else rest[0]

    @pl.when(kk == 0)
    def _():
        o_ref[...] = x_ref[...]

    def piece(rows):
        o_ref[rows, :] += jnp.dot(a_ref[rows, :], w_ref[...], preferred_element_type=F32)

    if dec is None:
        piece(pl.ds(0, o_ref.shape[0]))
    else:
        refs = rest[:n_in] + (rest[n_in],) + rest[n_in + 2:]
        lin = (pl.program_id(0) * pl.num_programs(1) + pl.program_id(1)) * pl.num_programs(2) + kk
        dec.run(lin, refs, n_host_steps, _row_pieces(o_ref.shape[0], piece))


def _down_proj(a, w_down, x, page_table, dec, tm, tn, tk):
    m, f = a.shape
    d = w_down.shape[1]
    grid = (m // tm, d // tn, f // tk)
    in_specs = [pl.BlockSpec((tm, tk), lambda i, j, k, pt: (i, k)),
                pl.BlockSpec((tk, tn), lambda i, j, k, pt: (k, j)),
                pl.BlockSpec((tm, tn), lambda i, j, k, pt: (i, j))]
    out_specs = [pl.BlockSpec((tm, tn), lambda i, j, k, pt: (i, j))]
    out_shape = [jax.ShapeDtypeStruct((m, d), F32)]
    args, scratch = [a, w_down, x], []
    n_host = grid[0] * grid[1] * grid[2]
    if dec is not None:
        assert n_host >= dec.n_lin
        d_in, d_out = dec.specs(lambda i, j, k: (i * grid[1] + j) * grid[2] + k)
        in_specs += d_in
        out_specs = [d_out] + out_specs
        out_shape = [dec.out_shape] + out_shape
        args += dec.args
        scratch = dec.scratch
    res = pl.pallas_call(
        functools.partial(_down_kernel, dec=dec, n_host_steps=n_host),
        grid_spec=pltpu.PrefetchScalarGridSpec(num_scalar_prefetch=1, grid=grid, in_specs=in_specs,
                                               out_specs=out_specs, scratch_shapes=scratch),
        out_shape=out_shape,
        compiler_params=_params("arbitrary", "arbitrary", "arbitrary"),
    )(page_table, *args)
    return (res[1], res[0]) if dec is not None else (res[0], None)


def _layer(xp, xs, ck, cvv, st, page_table, w, layer, slopes):
    bsz, s, d = xp.shape
    db, t, _ = xs.shape
    mp, ms = bsz * s, db * t
    att_w = d // 2
    conv_ch = d - att_w
    nh = att_w // V_DIM
    hist = CONV_WIDTH - 1
    lam_init = 0.8 - 0.6 * math.exp(-0.3 * layer)
    lam_vecs, sg = w['lam_vecs'], w['subln_g']
    n_pages = page_table.shape[1]
    tmp, tms = _tile(mp, 1024), _tile(ms, 1024)
    assert tms == ms

    xs2 = xs.reshape(ms, d)
    hs = _rms_cast(xs2, w['attn_norm_g'], _tile(ms, 256))
    offs = (0, att_w, 2 * att_w, 3 * att_w, 3 * att_w + conv_ch)
    (qs, ks_f, _, vs_f, _, us), w_in_pieces = _in_proj(hs, w, [(w['w_in'], o) for o in offs], tms, 512, True)
    r4 = lambda a: a.reshape(db, t, nh, V_DIM)
    nb_a = db // 2
    mk_dec = lambda b0, nb: _Decode(r4(qs), r4(ks_f), r4(vs_f), ck, cvv, slopes, lam_vecs, sg, lam_init,
                                    n_pages, b0, nb)
    w_out_b, w_up_b, w_down_b = _cast_bf16(w['w_out']), _cast_bf16(w['w_up']), _cast_bf16(w['w_down'])

    xp2 = xp.reshape(mp, d)
    hp = _rms_cast(xp2, w['attn_norm_g'], _tile(mp, 256))
    (qp, kp_f, kp_b, vp_f, vp_b, upr), _ = _in_proj(hp, w, [(p, 0) for p in w_in_pieces], tmp, 1024, False)
    tq = _tile(s, 512)
    r3 = lambda a: a.reshape(bsz, s, att_w)
    att_p = _prompt_attention(r3(qp), r3(kp_b), r3(vp_b), slopes, lam_vecs, sg, lam_init, tq).reshape(mp, att_w)
    up3 = upr.reshape(bsz, s, conv_ch)
    conv_tt = _tile(s, 128)
    cv_p = _conv_module(up3, up3, conv_tt // HALO, True, w, conv_tt).reshape(mp, conv_ch)
    x1p = _out_proj(att_p, cv_p, w_out_b, xp2, tmp, 1024)
    h2p = _rms_cast(x1p, w['mlp_norm_g'], _tile(mp, 256))
    act_p, att_a = _up_proj(h2p, w_up_b, page_table, mk_dec(0, nb_a), tmp, 512)
    yp, att_b = _down_proj(act_p, w_down_b, x1p, page_table, mk_dec(nb_a, db - nb_a), tmp, 512,
                           _tile(w_down_b.shape[0], 4096))

    att_s = jnp.concatenate([att_a, att_b], axis=0).reshape(ms, att_w)
    us3 = us.reshape(db, t, conv_ch)
    halo_s = jnp.pad(st, ((0, 0), (HALO - hist, 0), (0, 0)))
    cv_s = _conv_module(us3, halo_s, 0, False, w, t).reshape(ms, conv_ch)
    x1s = _out_proj(att_s, cv_s, w_out_b, xs2, tms, 1024)
    h2s = _rms_cast(x1s, w['mlp_norm_g'], _tile(ms, 256))
    act_s, _ = _up_proj(h2s, w_up_b, page_table, None, tms, 1024)
    ys, _ = _down_proj(act_s, w_down_b, x1s, page_table, None, tms, 1024, _tile(w_down_b.shape[0], 4096))

    r5 = lambda a, b_, t_: a.reshape(b_, t_, nh, V_DIM)
    cs = jnp.concatenate([st, us3], axis=1)[:, -hist:]
    return (yp.reshape(bsz, s, d), ys.reshape(db, t, d),
            (r5(kp_f, bsz, s), r5(vp_f, bsz, s), up3[:, -hist:], r5(ks_f, db, t), r5(vs_f, db, t), cs))


def kernel(x_prompt, x_sample, cache_k, cache_v, state_conv, page_table, attn_norm_g, w_in, q_norm_g, k_norm_g, lambda_q1, lambda_k1, lambda_q2, lambda_k2, subln_g, conv_w, conv_b, conv_ln_g, conv_ln_b, w_out, mlp_norm_g, w_up, w_down):
    depth = w_in.shape[0]
    d = x_prompt.shape[-1]
    att_w = d // 2
    nh = att_w // V_DIM
    slopes = jnp.exp2(-8.0 * jnp.arange(1, nh + 1, dtype=F32) / nh)
    xp, xs = x_prompt, x_sample
    outs = [[] for _ in range(6)]
    for l in range(depth):
        w = {'attn_norm_g': attn_norm_g[l], 'w_in': w_in[l],
             'q_gain': jnp.tile(q_norm_g[l], att_w // HEAD_DIM).reshape(1, att_w),
             'k_gain': jnp.tile(k_norm_g[l], att_w // HEAD_DIM).reshape(1, att_w),
             'lam_vecs': [a[l].reshape(1, HEAD_DIM) for a in (lambda_q1, lambda_k1, lambda_q2, lambda_k2)],
             'subln_g': subln_g[l], 'conv_w': conv_w[l], 'conv_b': conv_b[l], 'conv_ln_g': conv_ln_g[l],
             'conv_ln_b': conv_ln_b[l], 'w_out': w_out[l], 'mlp_norm_g': mlp_norm_g[l], 'w_up': w_up[l],
             'w_down': w_down[l]}
        xp, xs, extra = _layer(xp, xs, cache_k[l], cache_v[l], state_conv[l], page_table, w, l, slopes)
        for lst, val in zip(outs, extra):
            lst.append(val)
    return (xp, xs) + tuple(jnp.stack(o) for o in outs)
```

```python
import functools
import math

import jax
import jax.numpy as jnp
from jax import lax
from jax.experimental import pallas as pl
from jax.experimental.pallas import tpu as pltpu

HEAD_DIM = 128
V_DIM = 2 * HEAD_DIM
CONV_WIDTH = 31
EPS = 1e-6
SCALE = HEAD_DIM ** -0.5
LOG2E = math.log2(math.e)
NEG = -1e30
M_FLOOR = -1e20
LANES = 128
SUBLANES = 8
VMEM_LIMIT = 56 * 1024 * 1024
VMEM_LIMIT_HOSTING = 62 * 1024 * 1024
HALO = 32
PAGES_PER_STEP = 4
CAST_BLOCK_BYTES = 8 * 1024 * 1024
HOST_PIECES = 4
BF16 = jnp.bfloat16
F32 = jnp.float32


def _params(*sem, vmem_limit=VMEM_LIMIT):
    return pltpu.CompilerParams(dimension_semantics=sem, vmem_limit_bytes=vmem_limit)


def _tile(n, pref):
    t = min(n, pref)
    assert n % t == 0
    return t


def _rms_cast_kernel(x_ref, g_ref, o_ref):
    x = x_ref[...]
    ms = jnp.mean(x * x, axis=-1, keepdims=True)
    o_ref[...] = ((x * lax.rsqrt(ms + EPS)) * g_ref[...]).astype(o_ref.dtype)


def _rms_cast(x, g, tm):
    m, d = x.shape
    return pl.pallas_call(
        _rms_cast_kernel,
        grid=(m // tm,),
        in_specs=[pl.BlockSpec((tm, d), lambda i: (i, 0)),
                  pl.BlockSpec((1, d), lambda i: (0, 0))],
        out_specs=pl.BlockSpec((tm, d), lambda i: (i, 0)),
        out_shape=jax.ShapeDtypeStruct((m, d), BF16),
        compiler_params=_params("parallel"),
    )(x, g.reshape(1, d))


def _cast_kernel(w_ref, o_ref):
    o_ref[...] = w_ref[...].astype(o_ref.dtype)


def _cast_bf16(w):
    r, c = w.shape
    tr = _tile(r, max(SUBLANES, CAST_BLOCK_BYTES // (4 * c)))
    return pl.pallas_call(
        _cast_kernel,
        grid=(r // tr,),
        in_specs=[pl.BlockSpec((tr, c), lambda i: (i, 0))],
        out_specs=pl.BlockSpec((tr, c), lambda i: (i, 0)),
        out_shape=jax.ShapeDtypeStruct((r, c), BF16),
        compiler_params=_params("parallel"),
    )(w)


def _group_rms(acc, g):
    outs = []
    for c in range(acc.shape[1] // HEAD_DIM):
        xg = acc[:, c * HEAD_DIM:(c + 1) * HEAD_DIM]
        ms = jnp.mean(xg * xg, axis=-1, keepdims=True)
        outs.append(xg * lax.rsqrt(ms + EPS))
    return jnp.concatenate(outs, axis=-1) * g


def _relu2(acc):
    r = jnp.maximum(acc, 0.0)
    return r * r


def _proj_kernel(*refs, n_w, n_out, epilogue, cast_w):
    h_ref, w_refs = refs[0], refs[1:1 + n_w]
    pos = 1 + n_w
    g_ref = None
    if epilogue == "norm":
        g_ref, pos = refs[pos], pos + 1
    o_refs = refs[pos:pos + n_out]
    wb_refs = refs[pos + n_out:]
    h = h_ref[...]
    accs = []
    for i, w_ref in enumerate(w_refs):
        w = w_ref[...]
        if cast_w:
            w = w.astype(BF16)
            wb_refs[i][...] = w
        accs.append(jnp.dot(h, w, preferred_element_type=F32))
    if epilogue == "norm":
        y = _group_rms(accs[0], g_ref[...])
    elif epilogue == "glu":
        y = accs[0] * jax.nn.sigmoid(accs[1])
    elif epilogue == "relu2":
        y = _relu2(accs[0])
    else:
        y = accs[0]
    for o_ref in o_refs:
        o_ref[...] = y.astype(o_ref.dtype)


def _proj(h, ws, n_cols, epilogue, out_dtypes, tm, tn, gain=None, cast_w=False):
    m, k = h.shape
    in_specs = [pl.BlockSpec((tm, k), lambda i, j: (i, 0))]
    args = [h]
    for arr, off in ws:
        in_specs.append(pl.BlockSpec((k, tn), lambda i, j, o=off // tn: (0, o + j)))
        args.append(arr)
    if gain is not None:
        in_specs.append(pl.BlockSpec((1, tn), lambda i, j: (0, j)))
        args.append(gain)
    out_specs = [pl.BlockSpec((tm, tn), lambda i, j: (i, j)) for _ in out_dtypes]
    out_shape = [jax.ShapeDtypeStruct((m, n_cols), dt) for dt in out_dtypes]
    if cast_w:
        assert m == tm
        out_specs += [pl.BlockSpec((k, tn), lambda i, j: (0, j)) for _ in ws]
        out_shape += [jax.ShapeDtypeStruct((k, n_cols), BF16) for _ in ws]
    kern = functools.partial(_proj_kernel, n_w=len(ws), n_out=len(out_dtypes), epilogue=epilogue, cast_w=cast_w)
    return pl.pallas_call(
        kern, grid=(m // tm, n_cols // tn), in_specs=in_specs, out_specs=out_specs, out_shape=out_shape,
        compiler_params=_params("parallel", "arbitrary"),
    )(*args)


def _in_proj(h, w, ws, tm, tn, cast_w):
    wq, wk, wv, wa, wg = ws
    qk_w = w['q_gain'].shape[1]
    res_q = _proj(h, [wq], qk_w, "norm", [BF16], tm, tn, gain=w['q_gain'], cast_w=cast_w)
    res_k = _proj(h, [wk], qk_w, "norm", [F32, BF16], tm, tn, gain=w['k_gain'], cast_w=cast_w)
    res_v = _proj(h, [wv], qk_w, "plain", [F32, BF16], tm, tn, cast_w=cast_w)
    res_u = _proj(h, [wa, wg], qk_w, "glu", [F32], tm, tn // 2, cast_w=cast_w)
    acts = (res_q[0], res_k[0], res_k[1], res_v[0], res_v[1], res_u[0])
    pieces = (res_q[1], res_k[2], res_v[2], res_u[1], res_u[2]) if cast_w else None
    return acts, pieces


def _conv_kernel(halo_ref, u_ref, w_ref, b_ref, lg_ref, lb_ref, o_ref, ext_s, sh_s, y_s,
                 *, zero_first, cchunk):
    tt, ch = u_ref.shape[1], u_ref.shape[2]
    ext_s[pl.ds(0, HALO), :] = halo_ref[0]
    if zero_first:
        @pl.when(pl.program_id(1) == 0)
        def _():
            ext_s[pl.ds(0, HALO), :] = jnp.zeros((HALO, ch), F32)
    ext_s[pl.ds(HALO, tt), :] = u_ref[0]
    first = HALO - (CONV_WIDTH - 1)
    span = sh_s.shape[1]

    def chunk(c, carry):
        c0 = pl.multiple_of(c * cchunk, cchunk)
        for r in range(1, SUBLANES):
            sh_s[r] = ext_s[pl.ds(r, span), pl.ds(c0, cchunk)]
        acc = jnp.broadcast_to(b_ref[:, pl.ds(c0, cchunk)], (tt, cchunk))
        for tap in range(CONV_WIDTH):
            r, base = (first + tap) % SUBLANES, (first + tap) // SUBLANES * SUBLANES
            src = ext_s[pl.ds(base, tt), pl.ds(c0, cchunk)] if r == 0 else sh_s[r, pl.ds(base, tt), :]
            acc = acc + src * w_ref[pl.ds(tap, 1), pl.ds(c0, cchunk)]
        y_s[:, pl.ds(c0, cchunk)] = acc
        return carry

    lax.fori_loop(0, ch // cchunk, chunk, 0)
    y = y_s[...]
    mu = jnp.mean(y, axis=-1, keepdims=True)
    d = y - mu
    var = jnp.mean(d * d, axis=-1, keepdims=True)
    yn = d * lax.rsqrt(var + EPS) * lg_ref[...] + lb_ref[...]
    o_ref[0] = (yn * jax.nn.sigmoid(yn)).astype(o_ref.dtype)


def _conv_module(u, halo, halo_blocks_per_tile, zero_first, w, tt):
    bsz, t, ch = u.shape
    hb = halo_blocks_per_tile
    cchunk = 2 * LANES
    kern = functools.partial(_conv_kernel, zero_first=zero_first, cchunk=cchunk)
    row = lambda: pl.BlockSpec((1, ch), lambda b, i: (0, 0))
    return pl.pallas_call(
        kern,
        grid=(bsz, t // tt),
        in_specs=[pl.BlockSpec((1, HALO, ch), lambda b, i: (b, jnp.maximum(i * hb - 1, 0), 0)),
                  pl.BlockSpec((1, tt, ch), lambda b, i: (b, i, 0)),
                  pl.BlockSpec((CONV_WIDTH, ch), lambda b, i: (0, 0)),
                  row(), row(), row()],
        out_specs=pl.BlockSpec((1, tt, ch), lambda b, i: (b, i, 0)),
        out_shape=jax.ShapeDtypeStruct((bsz, t, ch), BF16),
        scratch_shapes=[pltpu.VMEM((HALO + tt, ch), F32),
                        pltpu.VMEM((SUBLANES, HALO - SUBLANES + tt, cchunk), F32),
                        pltpu.VMEM((tt, ch), F32)],
        compiler_params=_params("parallel", "arbitrary"),
    )(halo, u, w['conv_w'], w['conv_b'].reshape(1, ch), w['conv_ln_g'].reshape(1, ch), w['conv_ln_b'].reshape(1, ch))


def _lambda_full(lq1, lk1, lq2, lk2, lam_init):
    return (jnp.exp(jnp.sum(lq1 * lk1, axis=-1, keepdims=True))
            - jnp.exp(jnp.sum(lq2 * lk2, axis=-1, keepdims=True)) + lam_init)


def _subln(o, g, lam_init):
    ms = jnp.mean(o * o, axis=-1, keepdims=True)
    return (o * lax.rsqrt(ms + EPS)) * g * (1.0 - lam_init)


def _prompt_attn_kernel(slope_ref, q_ref, k_ref, v_ref, bias_ref,
                        lq1_ref, lk1_ref, lq2_ref, lk2_ref, sg_ref, o_ref,
                        vt_s, m_s, l_s, a_s, acc_s, p_s, *, lam_init):
    h = pl.program_id(1)
    qi = pl.program_id(2)
    tq = q_ref.shape[1]
    tile_off = slope_ref[h] * (LOG2E * tq)

    @pl.when(qi == 0)
    def _():
        for j in range(vt_s.shape[0]):
            vt_s[j] = v_ref[0, pl.ds(j * tq, tq), :].astype(F32).T.astype(vt_s.dtype)

    q = q_ref[0]
    m_s[...] = jnp.full(m_s.shape, NEG, F32)
    l_s[...] = jnp.zeros(l_s.shape, F32)
    acc_s[...] = jnp.zeros(acc_s.shape, F32)

    def softmax_tile(j):
        r0 = pl.multiple_of(j * tq, tq)
        k = k_ref[0, pl.ds(r0, tq), :]
        bias = bias_ref[0, jnp.where(j == qi, 1, 0)]
        offs = -tile_off * (qi - j).astype(F32)
        ps, alphas = [], []
        for c in range(2):
            cols = pl.ds(c * tq, tq)
            t = lax.dot_general(k[:, c * HEAD_DIM:(c + 1) * HEAD_DIM], q[:, c * HEAD_DIM:(c + 1) * HEAD_DIM],
                                (((1,), (1,)), ((), ())), preferred_element_type=F32) * (SCALE * LOG2E) + bias
            m_prev = m_s[:, cols]
            m_new = jnp.maximum(m_prev, jnp.max(t, axis=0, keepdims=True) + offs)
            p = jnp.exp2(t - (m_new - offs))
            alpha = jnp.exp2(m_prev - m_new)
            l_s[:, cols] = alpha * l_s[:, cols] + jnp.sum(p, axis=0, keepdims=True)
            m_s[:, cols] = m_new
            ps.append(p.astype(p_s.dtype))
            alphas.append(alpha)
        return ps, alphas

    def stage(ps, alphas):
        for c in range(2):
            p_s[:, pl.ds(c * tq, tq)] = ps[c]
            a_s[:, pl.ds(c * tq, tq)] = alphas[c]

    def accumulate(j):
        acc_s[...] = a_s[...] * acc_s[...] + jnp.dot(vt_s[j], p_s[...], preferred_element_type=F32)

    stage(*softmax_tile(0))

    def body(j, carry):
        ps, alphas = softmax_tile(j)
        accumulate(j - 1)
        stage(ps, alphas)
        return carry

    lax.fori_loop(1, qi + 1, body, 0)
    accumulate(qi)

    lam = _lambda_full(lq1_ref[...], lk1_ref[...], lq2_ref[...], lk2_ref[...], lam_init)
    inv = 1.0 / l_s[...]
    ot = acc_s[:, pl.ds(0, tq)] * inv[:, :tq] - lam * (acc_s[:, pl.ds(tq, tq)] * inv[:, tq:])
    o_ref[0] = _subln(ot.T, sg_ref[...], lam_init).astype(o_ref.dtype)


def _prompt_attention(q, k, v, slopes, lam_vecs, subln_g, lam_init, tq):
    bsz, s, width = q.shape
    nh = width // V_DIM
    rel = (jnp.arange(tq, dtype=jnp.int32)[None, :] - jnp.arange(tq, dtype=jnp.int32)[:, None])
    b_open = (-LOG2E) * slopes[:, None, None] * rel.astype(F32)[None]
    bias = jnp.stack([b_open, jnp.where(rel[None] >= 0, b_open, NEG)], axis=1)
    vec = lambda n: pl.BlockSpec((1, n), lambda b, h, i, sl: (0, 0))
    grid_spec = pltpu.PrefetchScalarGridSpec(
        num_scalar_prefetch=1,
        grid=(bsz, nh, s // tq),
        in_specs=[pl.BlockSpec((1, tq, V_DIM), lambda b, h, i, sl: (b, i, h)),
                  pl.BlockSpec((1, s, V_DIM), lambda b, h, i, sl: (b, 0, h)),
                  pl.BlockSpec((1, s, V_DIM), lambda b, h, i, sl: (b, 0, h)),
                  pl.BlockSpec((1, 2, tq, tq), lambda b, h, i, sl: (h, 0, 0, 0)),
                  vec(HEAD_DIM), vec(HEAD_DIM), vec(HEAD_DIM), vec(HEAD_DIM), vec(V_DIM)],
        out_specs=pl.BlockSpec((1, tq, V_DIM), lambda b, h, i, sl: (b, i, h)),
        scratch_shapes=[pltpu.VMEM((s // tq, V_DIM, tq), BF16),
                        pltpu.VMEM((1, 2 * tq), F32),
                        pltpu.VMEM((1, 2 * tq), F32),
                        pltpu.VMEM((1, 2 * tq), F32),
                        pltpu.VMEM((V_DIM, 2 * tq), F32),
                        pltpu.VMEM((tq, 2 * tq), BF16)],
    )
    return pl.pallas_call(
        functools.partial(_prompt_attn_kernel, lam_init=lam_init),
        grid_spec=grid_spec,
        out_shape=jax.ShapeDtypeStruct((bsz, s, width), BF16),
        compiler_params=_params("parallel", "parallel", "arbitrary"),
    )(slopes, q, k, v, bias, *lam_vecs, subln_g.reshape(1, V_DIM))


def _diag_rows(x, diag):
    return jnp.sum(jnp.where(diag, x, 0.0), axis=0, keepdims=True)


def _decode_step(step, refs, host, *, n_group, n_steps, lam_init):
    k_refs = refs[:n_group]
    v_refs = refs[n_group:2 * n_group]
    (qt_ref, kn_ref, vn_ref, bias_ref, biasn_ref, coff_ref,
     lq1_ref, lk1_ref, lq2_ref, lk2_ref, sg_ref, o_ref, m_s, l_s, o_s) = refs[2 * n_group:]
    qt = qt_ref[0]
    coff = coff_ref[...]

    def scores(kf, bias):
        s = jnp.dot(kf.astype(BF16), qt, preferred_element_type=F32)
        return s * (SCALE * LOG2E) + bias

    def colmax(s):
        return jnp.max(s.reshape(-1, SUBLANES, LANES), axis=0)

    def probs(s, m):
        n = s.shape[0] // SUBLANES
        p = jnp.exp2(s.reshape(n, SUBLANES, LANES) - m[None])
        return jnp.sum(p, axis=0), p.reshape(s.shape).astype(BF16)

    def pv(p, vf):
        return lax.dot_general(p, vf.astype(BF16), (((0,), (0,)), ((), ())),
                               preferred_element_type=F32)

    n_slots = 2 * n_group
    slot_of = [i * n_slots // max(len(host), 1) for i in range(len(host))]

    def host_work(slot):
        for i, piece in enumerate(host):
            if slot_of[i] == slot:
                piece()

    bias = bias_ref[...]
    ss = []
    for g in range(n_group):
        host_work(g)
        ss.append(scores(k_refs[g][0], bias))
    m = jnp.full((SUBLANES, LANES), M_FLOOR, F32)
    for g, s in enumerate(ss):
        m = jnp.maximum(m, colmax(s) + float(g) * coff)
    l = jnp.zeros((SUBLANES, LANES), F32)
    o = jnp.zeros((LANES, V_DIM), F32)
    for g in range(n_group):
        host_work(n_group + g)
        lg, p = probs(ss[g], m - float(g) * coff)
        l = l + lg
        o = o + pv(p, v_refs[g][0])
    m_s[step] = m
    l_s[step] = l
    o_s[step] = o

    @pl.when(step == n_steps - 1)
    def _():
        sn = scores(kn_ref[0], biasn_ref[...])
        mn = jnp.maximum(colmax(sn), M_FLOOR)
        ln, pn = probs(sn, mn)
        m_s[n_steps] = mn
        l_s[n_steps] = ln
        o_s[n_steps] = pv(pn, vn_ref[0])

        row = lax.broadcasted_iota(jnp.int32, (SUBLANES, LANES), 0)
        col = lax.broadcasted_iota(jnp.int32, (SUBLANES, LANES), 1)
        diag = row == col % SUBLANES
        n_ent = n_steps + 1
        pages_before = [e * n_group for e in range(n_steps)] + [n_steps * n_group]
        mt = [m_s[e] + float(pages_before[e]) * coff for e in range(n_ent)]
        mx = mt[0]
        for e in range(1, n_ent):
            mx = jnp.maximum(mx, mt[e])
        wts = [jnp.exp2(mt[e] - mx) for e in range(n_ent)]
        lsum = wts[0] * l_s[0]
        for e in range(1, n_ent):
            lsum = lsum + wts[e] * l_s[e]
        inv = 1.0 / _diag_rows(lsum, diag)
        wrows = [_diag_rows(wts[e], diag) * inv for e in range(n_ent)]
        wmat = jnp.concatenate(wrows + [jnp.zeros((LANES - n_ent, LANES), F32)], axis=0)
        wt = wmat.T
        out = wt[:, 0:1] * o_s[0]
        for e in range(1, n_ent):
            out = out + wt[:, e:e + 1] * o_s[e]
        half = LANES // 2
        lam = _lambda_full(lq1_ref[...], lk1_ref[...], lq2_ref[...], lk2_ref[...], lam_init)
        od = out[:half] - lam * out[half:]
        o_ref[0] = _subln(od, sg_ref[...], lam_init).astype(o_ref.dtype)


class _Decode:
    def __init__(self, q, k_new, v_new, cache_k, cache_v, slopes, lam_vecs, subln_g, lam_init, n_pages, b0, nb):
        db, t, nh, _ = q.shape
        n_phys, page, _, _ = cache_k.shape
        self.n_group = n_group = min(PAGES_PER_STEP, n_pages)
        assert nh == SUBLANES and 2 * t * nh == LANES and n_pages % n_group == 0
        self.n_steps = n_steps = n_pages // n_group
        self.n_lin = nb * n_steps
        self.b0, self.nb, self.lam_init = b0, nb, lam_init
        rows_q = t * nh
        page_rows = page * nh
        ck = cache_k.reshape(n_phys, page_rows, V_DIM)
        cv = cache_v.reshape(n_phys, page_rows, V_DIM)
        q5 = q.reshape(db, t, nh, 2, HEAD_DIM)
        qt = jnp.einsum('bthcd,ce->bcdeth', q5, jnp.eye(2, dtype=q.dtype)).reshape(db, 2 * HEAD_DIM, LANES)
        kn = k_new.reshape(db, rows_q, V_DIM)
        vn = v_new.reshape(db, rows_q, V_DIM)
        sl2 = slopes * LOG2E
        col = jnp.arange(LANES)
        col_h, col_q = col % nh, (col // nh) % t
        rows = jnp.arange(page_rows)
        row_h, row_t = rows % nh, rows // nh
        bias = jnp.where(row_h[:, None] == col_h[None, :], (sl2[row_h] * row_t)[:, None], NEG).astype(F32)
        rn = jnp.arange(rows_q)
        rn_h, rn_t = rn % nh, rn // nh
        ok = (rn_h[:, None] == col_h[None, :]) & (rn_t[:, None] <= col_q[None, :])
        biasn = jnp.where(ok, (sl2[rn_h] * rn_t)[:, None], NEG).astype(F32)
        coff = jnp.broadcast_to((sl2[col_h] * page)[None, :], (SUBLANES, LANES)).astype(F32)
        self.args = ([ck] * n_group + [cv] * n_group
                     + [qt.astype(BF16), kn, vn, bias, biasn, coff, *lam_vecs, subln_g.reshape(1, V_DIM)])
        self.block_shapes = ([(1, page_rows, V_DIM)] * (2 * n_group)
                             + [(1, 2 * HEAD_DIM, LANES), (1, rows_q, V_DIM), (1, rows_q, V_DIM),
                                (page_rows, LANES), (rows_q, LANES), (SUBLANES, LANES),
                                (1, HEAD_DIM), (1, HEAD_DIM), (1, HEAD_DIM), (1, HEAD_DIM), (1, V_DIM)])
        self.out_block = (1, rows_q, V_DIM)
        self.out_shape = jax.ShapeDtypeStruct((nb, rows_q, V_DIM), BF16)
        self.scratch = [pltpu.VMEM((n_steps + 1, SUBLANES, LANES), F32),
                        pltpu.VMEM((n_steps + 1, SUBLANES, LANES), F32),
                        pltpu.VMEM((n_steps + 1, LANES, V_DIM), F32)]

    def specs(self, lin_of):
        n_group, n_steps, b0 = self.n_group, self.n_steps, self.b0
        last = self.n_lin - 1

        def seq(*idx):
            lin = jnp.minimum(lin_of(*idx[:-1]), last)
            return lin // n_steps, lin % n_steps

        def page_map(g):
            def f(*idx):
                b, ps = seq(*idx)
                return (idx[-1][b0 + b, ps * n_group + g], 0, 0)
            return f

        def per_b(shape, off):
            def f(*idx):
                return (off + seq(*idx)[0],) + (0,) * (len(shape) - 1)
            return f

        const = lambda shape: (lambda *idx: (0,) * len(shape))
        in_specs = []
        for i, shape in enumerate(self.block_shapes):
            if i < 2 * n_group:
                in_specs.append(pl.BlockSpec(shape, page_map(i % n_group)))
            elif i < 2 * n_group + 3:
                in_specs.append(pl.BlockSpec(shape, per_b(shape, b0)))
            else:
                in_specs.append(pl.BlockSpec(shape, const(shape)))
        out_spec = pl.BlockSpec(self.out_block, per_b(self.out_block, 0))
        return in_specs, out_spec

    def run(self, lin, refs, n_host_steps, host):
        kw = dict(n_group=self.n_group, n_steps=self.n_steps, lam_init=self.lam_init)
        if n_host_steps == self.n_lin:
            _decode_step(lin % self.n_steps, refs, host, **kw)
        else:
            for piece in host:
                piece()

            @pl.when(lin < self.n_lin)
            def _():
                _decode_step(lin % self.n_steps, refs, [], **kw)


def _row_pieces(tm, piece_fn):
    rows = tm // HOST_PIECES
    return [functools.partial(piece_fn, pl.ds(r * rows, rows)) for r in range(HOST_PIECES)]


def _out_proj_kernel(att_ref, cv_ref, wa_ref, wc_ref, x_ref, o_ref):
    acc = jnp.dot(att_ref[...], wa_ref[...], preferred_element_type=F32)
    acc = acc + jnp.dot(cv_ref[...], wc_ref[...], preferred_element_type=F32)
    o_ref[...] = x_ref[...] + acc


def _out_proj(att, cv, w_out, x, tm, tn):
    m, half = att.shape
    d = x.shape[1]
    return pl.pallas_call(
        _out_proj_kernel,
        grid=(m // tm, d // tn),
        in_specs=[pl.BlockSpec((tm, half), lambda i, j: (i, 0)),
                  pl.BlockSpec((tm, half), lambda i, j: (i, 0)),
                  pl.BlockSpec((half, tn), lambda i, j: (0, j)),
                  pl.BlockSpec((half, tn), lambda i, j: (1, j)),
                  pl.BlockSpec((tm, tn), lambda i, j: (i, j))],
        out_specs=pl.BlockSpec((tm, tn), lambda i, j: (i, j)),
        out_shape=jax.ShapeDtypeStruct((m, d), F32),
        compiler_params=_params("parallel", "arbitrary"),
    )(att, cv, w_out, w_out, x)


def _up_kernel(pt_ref, h_ref, w_ref, *rest, dec, n_host_steps, cast_w):
    n_in = len(dec.block_shapes) if dec is not None else 0
    o_ref = rest[n_in + 1] if dec is not None else rest[0]
    if cast_w:
        rest, wb_ref = rest[:-1], rest[-1]
        wb_ref[...] = w_ref[...].astype(BF16)
    else:
        wb_ref = w_ref

    def piece(rows):
        acc = jnp.dot(h_ref[rows, :], wb_ref[...], preferred_element_type=F32)
        o_ref[rows, :] = _relu2(acc).astype(o_ref.dtype)

    if dec is None:
        piece(pl.ds(0, o_ref.shape[0]))
    else:
        refs = rest[:n_in] + (rest[n_in],) + rest[n_in + 2:]
        lin = pl.program_id(0) * pl.num_programs(1) + pl.program_id(1)
        dec.run(lin, refs, n_host_steps, _row_pieces(o_ref.shape[0], piece))


def _up_proj(h, w_up, page_table, dec, tm, tn):
    m, k = h.shape
    n = w_up.shape[1]
    grid = (m // tm, n // tn)
    in_specs = [pl.BlockSpec((tm, k), lambda i, j, pt: (i, 0)),
                pl.BlockSpec((k, tn), lambda i, j, pt: (0, j))]
    out_specs = [pl.BlockSpec((tm, tn), lambda i, j, pt: (i, j))]
    out_shape = [jax.ShapeDtypeStruct((m, n), BF16)]
    args, scratch = [h, w_up], []
    if dec is not None:
        assert grid[0] * grid[1] >= dec.n_lin
        d_in, d_out = dec.specs(lambda i, j: i * grid[1] + j)
        in_specs += d_in
        out_specs = [d_out] + out_specs
        out_shape = [dec.out_shape] + out_shape
        args += dec.args
        scratch = list(dec.scratch)
    cast_w = w_up.dtype != BF16
    if cast_w:
        scratch = scratch + [pltpu.VMEM((k, tn), BF16)]
    res = pl.pallas_call(
        functools.partial(_up_kernel, dec=dec, n_host_steps=grid[0] * grid[1], cast_w=cast_w),
        grid_spec=pltpu.PrefetchScalarGridSpec(num_scalar_prefetch=1, grid=grid, in_specs=in_specs,
                                               out_specs=out_specs, scratch_shapes=scratch),
        out_shape=out_shape,
        compiler_params=_params("arbitrary", "arbitrary", vmem_limit=VMEM_LIMIT_HOSTING),
    )(page_table, *args)
    return (res[1], res[0]) if dec is not None else (res[0], None)


def _down_kernel(pt_ref, a_ref, w_ref, x_ref, *rest, dec, n_host_steps):
    kk = pl.program_id(2)
    n_in = len(dec.block_shapes) if dec is not None else 0
    o_ref = rest[n_in + 1] if dec is not None else rest[0]

    @pl.when(kk == 0)
    def _():
        o_ref[...] = x_ref[...]

    def piece(rows):
        o_ref[rows, :] += jnp.dot(a_ref[rows, :], w_ref[...], preferred_element_type=F32)

    if dec is None:
        piece(pl.ds(0, o_ref.shape[0]))
    else:
        refs = rest[:n_in] + (rest[n_in],) + rest[n_in + 2:]
        lin = (pl.program_id(0) * pl.num_programs(1) + pl.program_id(1)) * pl.num_programs(2) + kk
        dec.run(lin, refs, n_host_steps, _row_pieces(o_ref.shape[0], piece))


def _down_proj(a, w_down, x, page_table, dec, tm, tn, tk):
    m, f = a.shape
    d = w_down.shape[1]
    grid = (m // tm, d // tn, f // tk)
    in_specs = [pl.BlockSpec((tm, tk), lambda i, j, k, pt: (i, k)),
                pl.BlockSpec((tk, tn), lambda i, j, k, pt: (k, j)),
                pl.BlockSpec((tm, tn), lambda i, j, k, pt: (i, j))]
    out_specs = [pl.BlockSpec((tm, tn), lambda i, j, k, pt: (i, j))]
    out_shape = [jax.ShapeDtypeStruct((m, d), F32)]
    args, scratch = [a, w_down, x], []
    n_host = grid[0] * grid[1] * grid[2]
    if dec is not None:
        assert n_host >= dec.n_lin
        d_in, d_out = dec.specs(lambda i, j, k: (i * grid[1] + j) * grid[2] + k)
        in_specs += d_in
        out_specs = [d_out] + out_specs
        out_shape = [dec.out_shape] + out_shape
        args += dec.args
        scratch = dec.scratch
    res = pl.pallas_call(
        functools.partial(_down_kernel, dec=dec, n_host_steps=n_host),
        grid_spec=pltpu.PrefetchScalarGridSpec(num_scalar_prefetch=1, grid=grid, in_specs=in_specs,
                                               out_specs=out_specs, scratch_shapes=scratch),
        out_shape=out_shape,
        compiler_params=_params("arbitrary", "arbitrary", "arbitrary"),
    )(page_table, *args)
    return (res[1], res[0]) if dec is not None else (res[0], None)


def _layer(xp, xs, ck, cvv, st, page_table, w, layer, slopes):
    bsz, s, d = xp.shape
    db, t, _ = xs.shape
    mp, ms = bsz * s, db * t
    att_w = d // 2
    conv_ch = d - att_w
    nh = att_w // V_DIM
    hist = CONV_WIDTH - 1
    lam_init = 0.8 - 0.6 * math.exp(-0.3 * layer)
    lam_vecs, sg = w['lam_vecs'], w['subln_g']
    n_pages = page_table.shape[1]
    tmp, tms = _tile(mp, 1024), _tile(ms, 1024)
    assert tms == ms

    xs2 = xs.reshape(ms, d)
    hs = _rms_cast(xs2, w['attn_norm_g'], _tile(ms, 256))
    offs = (0, att_w, 2 * att_w, 3 * att_w, 3 * att_w + conv_ch)
    (qs, ks_f, _, vs_f, _, us), w_in_pieces = _in_proj(hs, w, [(w['w_in'], o) for o in offs], tms, 512, True)
    r4 = lambda a: a.reshape(db, t, nh, V_DIM)
    nb_a = db // 2
    mk_dec = lambda b0, nb: _Decode(r4(qs), r4(ks_f), r4(vs_f), ck, cvv, slopes, lam_vecs, sg, lam_init,
                                    n_pages, b0, nb)
    w_out_b, w_down_b = _cast_bf16(w['w_out']), _cast_bf16(w['w_down'])

    xp2 = xp.reshape(mp, d)
    hp = _rms_cast(xp2, w['attn_norm_g'], _tile(mp, 256))
    (qp, kp_f, kp_b, vp_f, vp_b, upr), _ = _in_proj(hp, w, [(p, 0) for p in w_in_pieces], tmp, 1024, False)
    tq = _tile(s, 512)
    r3 = lambda a: a.reshape(bsz, s, att_w)
    att_p = _prompt_attention(r3(qp), r3(kp_b), r3(vp_b), slopes, lam_vecs, sg, lam_init, tq).reshape(mp, att_w)
    up3 = upr.reshape(bsz, s, conv_ch)
    conv_tt = _tile(s, 128)
    cv_p = _conv_module(up3, up3, conv_tt // HALO, True, w, conv_tt).reshape(mp, conv_ch)
    x1p = _out_proj(att_p, cv_p, w_out_b, xp2, tmp, 1024)
    h2p = _rms_cast(x1p, w['mlp_norm_g'], _tile(mp, 256))
    act_p, att_a = _up_proj(h2p, w['w_up'], page_table, mk_dec(0, nb_a), tmp, 512)
    yp, att_b = _down_proj(act_p, w_down_b, x1p, page_table, mk_dec(nb_a, db - nb_a), tmp, 1024,
                           _tile(w_down_b.shape[0], 2048))

    att_s = jnp.concatenate([att_a, att_b], axis=0).reshape(ms, att_w)
    us3 = us.reshape(db, t, conv_ch)
    halo_s = jnp.pad(st, ((0, 0), (HALO - hist, 0), (0, 0)))
    cv_s = _conv_module(us3, halo_s, 0, False, w, t).reshape(ms, conv_ch)
    x1s = _out_proj(att_s, cv_s, w_out_b, xs2, tms, 1024)
    h2s = _rms_cast(x1s, w['mlp_norm_g'], _tile(ms, 256))
    act_s, _ = _up_proj(h2s, w['w_up'], page_table, None, tms, 512)
    ys, _ = _down_proj(act_s, w_down_b, x1s, page_table, None, tms, 1024, _tile(w_down_b.shape[0], 4096))

    r5 = lambda a, b_, t_: a.reshape(b_, t_, nh, V_DIM)
    cs = jnp.concatenate([st, us3], axis=1)[:, -hist:]
    return (yp.reshape(bsz, s, d), ys.reshape(db, t, d),
            (r5(kp_f, bsz, s), r5(vp_f, bsz, s), up3[:, -hist:], r5(ks_f, db, t), r5(vs_f, db, t), cs))


def kernel(x_prompt, x_sample, cache_k, cache_v, state_conv, page_table, attn_norm_g, w_in, q_norm_g, k_norm_g, lambda_q1, lambda_k1, lambda_q2, lambda_k2, subln_g, conv_w, conv_b, conv_ln_g, conv_ln_b, w_out, mlp_norm_g, w_up, w_down):
    depth = w_in.shape[0]
    d = x_prompt.shape[-1]
    att_w = d // 2
    nh = att_w // V_DIM
    slopes = jnp.exp2(-8.0 * jnp.arange(1, nh + 1, dtype=F32) / nh)
    xp, xs = x_prompt, x_sample
    outs = [[] for _ in range(6)]
    for l in range(depth):
        w = {'attn_norm_g': attn_norm_g[l], 'w_in': w_in[l],
             'q_gain': jnp.tile(q_norm_g[l], att_w // HEAD_DIM).reshape(1, att_w),
             'k_gain': jnp.tile(k_norm_g[l], att_w // HEAD_DIM).reshape(1, att_w),
             'lam_vecs': [a[l].reshape(1, HEAD_DIM) for a in (lambda_q1, lambda_k1, lambda_q2, lambda_k2)],
             'subln_g': subln_g[l], 'conv_w': conv_w[l], 'conv_b': conv_b[l], 'conv_ln_g': conv_ln_g[l],
             'conv_ln_b': conv_ln_b[l], 'w_out': w_out[l], 'mlp_norm_g': mlp_norm_g[l], 'w_up': w_up[l],
             'w_down': w_down[l]}
        xp, xs, extra = _layer(xp, xs, cache_k[l], cache_v[l], state_conv[l], page_table, w, l, slopes)
        for lst, val in zip(outs, extra):
            lst.append(val)
    return (xp, xs) + tuple(jnp.stack(o) for o in outs)
```

```python
import functools
import math

import jax
import jax.numpy as jnp
from jax import lax
from jax.experimental import pallas as pl
from jax.experimental.pallas import tpu as pltpu

HEAD_DIM = 128
V_DIM = 2 * HEAD_DIM
CONV_WIDTH = 31
EPS = 1e-6
SCALE = HEAD_DIM ** -0.5
LOG2E = math.log2(math.e)
NEG = -1e30
M_FLOOR = -1e20
LANES = 128
SUBLANES = 8
VMEM_LIMIT = 56 * 1024 * 1024
VMEM_LIMIT_HOSTING = 62 * 1024 * 1024
HALO = 32
PAGES_PER_STEP = 4
ENTRIES_PER_STEP = 2
CAST_BLOCK_BYTES = 8 * 1024 * 1024
HOST_PIECES = 4
BF16 = jnp.bfloat16
F32 = jnp.float32


def _params(*sem, vmem_limit=VMEM_LIMIT):
    return pltpu.CompilerParams(dimension_semantics=sem, vmem_limit_bytes=vmem_limit)


def _tile(n, pref):
    t = min(n, pref)
    assert n % t == 0
    return t


def _rms_cast_kernel(x_ref, g_ref, o_ref):
    x = x_ref[...]
    ms = jnp.mean(x * x, axis=-1, keepdims=True)
    o_ref[...] = ((x * lax.rsqrt(ms + EPS)) * g_ref[...]).astype(o_ref.dtype)


def _rms_cast(x, g, tm):
    m, d = x.shape
    return pl.pallas_call(
        _rms_cast_kernel,
        grid=(m // tm,),
        in_specs=[pl.BlockSpec((tm, d), lambda i: (i, 0)),
                  pl.BlockSpec((1, d), lambda i: (0, 0))],
        out_specs=pl.BlockSpec((tm, d), lambda i: (i, 0)),
        out_shape=jax.ShapeDtypeStruct((m, d), BF16),
        compiler_params=_params("parallel"),
    )(x, g.reshape(1, d))


def _cast_kernel(w_ref, o_ref):
    o_ref[...] = w_ref[...].astype(o_ref.dtype)


def _cast_bf16(w):
    r, c = w.shape
    tr = _tile(r, max(SUBLANES, CAST_BLOCK_BYTES // (4 * c)))
    return pl.pallas_call(
        _cast_kernel,
        grid=(r // tr,),
        in_specs=[pl.BlockSpec((tr, c), lambda i: (i, 0))],
        out_specs=pl.BlockSpec((tr, c), lambda i: (i, 0)),
        out_shape=jax.ShapeDtypeStruct((r, c), BF16),
        compiler_params=_params("parallel"),
    )(w)


def _group_rms(acc, g):
    outs = []
    for c in range(acc.shape[1] // HEAD_DIM):
        xg = acc[:, c * HEAD_DIM:(c + 1) * HEAD_DIM]
        ms = jnp.mean(xg * xg, axis=-1, keepdims=True)
        outs.append(xg * lax.rsqrt(ms + EPS))
    return jnp.concatenate(outs, axis=-1) * g


def _relu2(acc):
    r = jnp.maximum(acc, 0.0)
    return r * r


def _proj_kernel(*refs, n_w, n_out, epilogue, cast_w):
    h_ref, w_refs = refs[0], refs[1:1 + n_w]
    pos = 1 + n_w
    g_ref = None
    if epilogue == "norm":
        g_ref, pos = refs[pos], pos + 1
    o_refs = refs[pos:pos + n_out]
    wb_refs = refs[pos + n_out:]
    h = h_ref[...]
    accs = []
    for i, w_ref in enumerate(w_refs):
        w = w_ref[...]
        if cast_w:
            w = w.astype(BF16)
            wb_refs[i][...] = w
        accs.append(jnp.dot(h, w, preferred_element_type=F32))
    if epilogue == "norm":
        y = _group_rms(accs[0], g_ref[...])
    elif epilogue == "glu":
        y = accs[0] * jax.nn.sigmoid(accs[1])
    elif epilogue == "relu2":
        y = _relu2(accs[0])
    else:
        y = accs[0]
    for o_ref in o_refs:
        o_ref[...] = y.astype(o_ref.dtype)


def _proj(h, ws, n_cols, epilogue, out_dtypes, tm, tn, gain=None, cast_w=False):
    m, k = h.shape
    in_specs = [pl.BlockSpec((tm, k), lambda i, j: (i, 0))]
    args = [h]
    for arr, off in ws:
        in_specs.append(pl.BlockSpec((k, tn), lambda i, j, o=off // tn: (0, o + j)))
        args.append(arr)
    if gain is not None:
        in_specs.append(pl.BlockSpec((1, tn), lambda i, j: (0, j)))
        args.append(gain)
    out_specs = [pl.BlockSpec((tm, tn), lambda i, j: (i, j)) for _ in out_dtypes]
    out_shape = [jax.ShapeDtypeStruct((m, n_cols), dt) for dt in out_dtypes]
    if cast_w:
        assert m == tm
        out_specs += [pl.BlockSpec((k, tn), lambda i, j: (0, j)) for _ in ws]
        out_shape += [jax.ShapeDtypeStruct((k, n_cols), BF16) for _ in ws]
    kern = functools.partial(_proj_kernel, n_w=len(ws), n_out=len(out_dtypes), epilogue=epilogue, cast_w=cast_w)
    return pl.pallas_call(
        kern, grid=(m // tm, n_cols // tn), in_specs=in_specs, out_specs=out_specs, out_shape=out_shape,
        compiler_params=_params("parallel", "arbitrary"),
    )(*args)


def _in_proj(h, w, ws, tm, tn, cast_w):
    wq, wk, wv, wa, wg = ws
    qk_w = w['q_gain'].shape[1]
    res_q = _proj(h, [wq], qk_w, "norm", [BF16], tm, tn, gain=w['q_gain'], cast_w=cast_w)
    res_k = _proj(h, [wk], qk_w, "norm", [F32, BF16], tm, tn, gain=w['k_gain'], cast_w=cast_w)
    res_v = _proj(h, [wv], qk_w, "plain", [F32, BF16], tm, tn, cast_w=cast_w)
    res_u = _proj(h, [wa, wg], qk_w, "glu", [F32], tm, tn // 2, cast_w=cast_w)
    acts = (res_q[0], res_k[0], res_k[1], res_v[0], res_v[1], res_u[0])
    pieces = (res_q[1], res_k[2], res_v[2], res_u[1], res_u[2]) if cast_w else None
    return acts, pieces


def _conv_kernel(halo_ref, u_ref, w_ref, b_ref, lg_ref, lb_ref, o_ref, ext_s, sh_s, y_s,
                 *, zero_first, cchunk):
    tt, ch = u_ref.shape[1], u_ref.shape[2]
    ext_s[pl.ds(0, HALO), :] = halo_ref[0]
    if zero_first:
        @pl.when(pl.program_id(1) == 0)
        def _():
            ext_s[pl.ds(0, HALO), :] = jnp.zeros((HALO, ch), F32)
    ext_s[pl.ds(HALO, tt), :] = u_ref[0]
    first = HALO - (CONV_WIDTH - 1)
    span = sh_s.shape[1]

    def chunk(c, carry):
        c0 = pl.multiple_of(c * cchunk, cchunk)
        for r in range(1, SUBLANES):
            sh_s[r] = ext_s[pl.ds(r, span), pl.ds(c0, cchunk)]
        acc = jnp.broadcast_to(b_ref[:, pl.ds(c0, cchunk)], (tt, cchunk))
        for tap in range(CONV_WIDTH):
            r, base = (first + tap) % SUBLANES, (first + tap) // SUBLANES * SUBLANES
            src = ext_s[pl.ds(base, tt), pl.ds(c0, cchunk)] if r == 0 else sh_s[r, pl.ds(base, tt), :]
            acc = acc + src * w_ref[pl.ds(tap, 1), pl.ds(c0, cchunk)]
        y_s[:, pl.ds(c0, cchunk)] = acc
        return carry

    lax.fori_loop(0, ch // cchunk, chunk, 0)
    y = y_s[...]
    mu = jnp.mean(y, axis=-1, keepdims=True)
    d = y - mu
    var = jnp.mean(d * d, axis=-1, keepdims=True)
    yn = d * lax.rsqrt(var + EPS) * lg_ref[...] + lb_ref[...]
    o_ref[0] = (yn * jax.nn.sigmoid(yn)).astype(o_ref.dtype)


def _conv_module(u, halo, halo_blocks_per_tile, zero_first, w, tt):
    bsz, t, ch = u.shape
    hb = halo_blocks_per_tile
    cchunk = 2 * LANES
    kern = functools.partial(_conv_kernel, zero_first=zero_first, cchunk=cchunk)
    row = lambda: pl.BlockSpec((1, ch), lambda b, i: (0, 0))
    return pl.pallas_call(
        kern,
        grid=(bsz, t // tt),
        in_specs=[pl.BlockSpec((1, HALO, ch), lambda b, i: (b, jnp.maximum(i * hb - 1, 0), 0)),
                  pl.BlockSpec((1, tt, ch), lambda b, i: (b, i, 0)),
                  pl.BlockSpec((CONV_WIDTH, ch), lambda b, i: (0, 0)),
                  row(), row(), row()],
        out_specs=pl.BlockSpec((1, tt, ch), lambda b, i: (b, i, 0)),
        out_shape=jax.ShapeDtypeStruct((bsz, t, ch), BF16),
        scratch_shapes=[pltpu.VMEM((HALO + tt, ch), F32),
                        pltpu.VMEM((SUBLANES, HALO - SUBLANES + tt, cchunk), F32),
                        pltpu.VMEM((tt, ch), F32)],
        compiler_params=_params("parallel", "arbitrary"),
    )(halo, u, w['conv_w'], w['conv_b'].reshape(1, ch), w['conv_ln_g'].reshape(1, ch), w['conv_ln_b'].reshape(1, ch))


def _lambda_full(lq1, lk1, lq2, lk2, lam_init):
    return (jnp.exp(jnp.sum(lq1 * lk1, axis=-1, keepdims=True))
            - jnp.exp(jnp.sum(lq2 * lk2, axis=-1, keepdims=True)) + lam_init)


def _subln(o, g, lam_init):
    ms = jnp.mean(o * o, axis=-1, keepdims=True)
    return (o * lax.rsqrt(ms + EPS)) * g * (1.0 - lam_init)


def _prompt_attn_kernel(slope_ref, q_ref, k_ref, v_ref, bias_ref,
                        lq1_ref, lk1_ref, lq2_ref, lk2_ref, sg_ref, o_ref,
                        vt_s, m_s, l_s, a_s, acc_s, p_s, *, lam_init):
    h = pl.program_id(1)
    qi = pl.program_id(2)
    tq = q_ref.shape[1]
    tile_off = slope_ref[h] * (LOG2E * tq)

    @pl.when(qi == 0)
    def _():
        for j in range(vt_s.shape[0]):
            vt_s[j] = v_ref[0, pl.ds(j * tq, tq), :].astype(F32).T.astype(vt_s.dtype)

    q = q_ref[0]
    m_s[...] = jnp.full(m_s.shape, NEG, F32)
    l_s[...] = jnp.zeros(l_s.shape, F32)
    acc_s[...] = jnp.zeros(acc_s.shape, F32)

    def softmax_tile(j):
        r0 = pl.multiple_of(j * tq, tq)
        k = k_ref[0, pl.ds(r0, tq), :]
        bias = bias_ref[0, jnp.where(j == qi, 1, 0)]
        offs = -tile_off * (qi - j).astype(F32)
        ps, alphas = [], []
        for c in range(2):
            cols = pl.ds(c * tq, tq)
            t = lax.dot_general(k[:, c * HEAD_DIM:(c + 1) * HEAD_DIM], q[:, c * HEAD_DIM:(c + 1) * HEAD_DIM],
                                (((1,), (1,)), ((), ())), preferred_element_type=F32) * (SCALE * LOG2E) + bias
            m_prev = m_s[:, cols]
            m_new = jnp.maximum(m_prev, jnp.max(t, axis=0, keepdims=True) + offs)
            p = jnp.exp2(t - (m_new - offs))
            alpha = jnp.exp2(m_prev - m_new)
            l_s[:, cols] = alpha * l_s[:, cols] + jnp.sum(p, axis=0, keepdims=True)
            m_s[:, cols] = m_new
            ps.append(p.astype(p_s.dtype))
            alphas.append(alpha)
        return ps, alphas

    def stage(ps, alphas):
        for c in range(2):
            p_s[:, pl.ds(c * tq, tq)] = ps[c]
            a_s[:, pl.ds(c * tq, tq)] = alphas[c]

    def accumulate(j):
        acc_s[...] = a_s[...] * acc_s[...] + jnp.dot(vt_s[j], p_s[...], preferred_element_type=F32)

    stage(*softmax_tile(0))

    def body(j, carry):
        ps, alphas = softmax_tile(j)
        accumulate(j - 1)
        stage(ps, alphas)
        return carry

    lax.fori_loop(1, qi + 1, body, 0)
    accumulate(qi)

    lam = _lambda_full(lq1_ref[...], lk1_ref[...], lq2_ref[...], lk2_ref[...], lam_init)
    inv = 1.0 / l_s[...]
    ot = acc_s[:, pl.ds(0, tq)] * inv[:, :tq] - lam * (acc_s[:, pl.ds(tq, tq)] * inv[:, tq:])
    o_ref[0] = _subln(ot.T, sg_ref[...], lam_init).astype(o_ref.dtype)


def _prompt_attention(q, k, v, slopes, lam_vecs, subln_g, lam_init, tq):
    bsz, s, width = q.shape
    nh = width // V_DIM
    rel = (jnp.arange(tq, dtype=jnp.int32)[None, :] - jnp.arange(tq, dtype=jnp.int32)[:, None])
    b_open = (-LOG2E) * slopes[:, None, None] * rel.astype(F32)[None]
    bias = jnp.stack([b_open, jnp.where(rel[None] >= 0, b_open, NEG)], axis=1)
    vec = lambda n: pl.BlockSpec((1, n), lambda b, h, i, sl: (0, 0))
    grid_spec = pltpu.PrefetchScalarGridSpec(
        num_scalar_prefetch=1,
        grid=(bsz, nh, s // tq),
        in_specs=[pl.BlockSpec((1, tq, V_DIM), lambda b, h, i, sl: (b, i, h)),
                  pl.BlockSpec((1, s, V_DIM), lambda b, h, i, sl: (b, 0, h)),
                  pl.BlockSpec((1, s, V_DIM), lambda b, h, i, sl: (b, 0, h)),
                  pl.BlockSpec((1, 2, tq, tq), lambda b, h, i, sl: (h, 0, 0, 0)),
                  vec(HEAD_DIM), vec(HEAD_DIM), vec(HEAD_DIM), vec(HEAD_DIM), vec(V_DIM)],
        out_specs=pl.BlockSpec((1, tq, V_DIM), lambda b, h, i, sl: (b, i, h)),
        scratch_shapes=[pltpu.VMEM((s // tq, V_DIM, tq), BF16),
                        pltpu.VMEM((1, 2 * tq), F32),
                        pltpu.VMEM((1, 2 * tq), F32),
                        pltpu.VMEM((1, 2 * tq), F32),
                        pltpu.VMEM((V_DIM, 2 * tq), F32),
                        pltpu.VMEM((tq, 2 * tq), BF16)],
    )
    return pl.pallas_call(
        functools.partial(_prompt_attn_kernel, lam_init=lam_init),
        grid_spec=grid_spec,
        out_shape=jax.ShapeDtypeStruct((bsz, s, width), BF16),
        compiler_params=_params("parallel", "parallel", "arbitrary"),
    )(slopes, q, k, v, bias, *lam_vecs, subln_g.reshape(1, V_DIM))


def _diag_rows(x, diag):
    return jnp.sum(jnp.where(diag, x, 0.0), axis=0, keepdims=True)


def _decode_step(step, refs, host, *, n_group, n_sub, n_steps, lam_init):
    k_refs = refs[:n_group]
    v_refs = refs[n_group:2 * n_group]
    qt_ref, kn_ref, vn_ref, sc_ref, vc_ref, o_ref, m_s, l_s, o_s = refs[2 * n_group:]
    page_rows, rows_q = k_refs[0].shape[1], kn_ref.shape[1]
    qt = qt_ref[0]
    coff = sc_ref[pl.ds(page_rows + rows_q, SUBLANES), :]

    def scores(kf, bias):
        s = jnp.dot(kf.astype(BF16), qt, preferred_element_type=F32)
        return s * (SCALE * LOG2E) + bias

    def colmax(s):
        return jnp.max(s.reshape(-1, SUBLANES, LANES), axis=0)

    def probs(s, m):
        n = s.shape[0] // SUBLANES
        p = jnp.exp2(s.reshape(n, SUBLANES, LANES) - m[None])
        return jnp.sum(p, axis=0), p.reshape(s.shape).astype(BF16)

    def pv(p, vf):
        return lax.dot_general(p, vf.astype(BF16), (((0,), (0,)), ((), ())),
                               preferred_element_type=F32)

    n_slots = 2 * n_group
    slot_of = [i * n_slots // max(len(host), 1) for i in range(len(host))]

    def host_work(slot):
        for i, piece in enumerate(host):
            if slot_of[i] == slot:
                piece()

    sub = n_group // n_sub
    bias = sc_ref[pl.ds(0, page_rows), :]
    ss = []
    for g in range(n_group):
        host_work(g)
        ss.append(scores(k_refs[g][0], bias))
    ms = []
    for e in range(n_sub):
        m = jnp.full((SUBLANES, LANES), M_FLOOR, F32)
        for i in range(sub):
            m = jnp.maximum(m, colmax(ss[e * sub + i]) + float(i) * coff)
        ms.append(m)
    for e in range(n_sub):
        l = jnp.zeros((SUBLANES, LANES), F32)
        o = jnp.zeros((LANES, V_DIM), F32)
        for i in range(sub):
            g = e * sub + i
            host_work(n_group + g)
            lg, p = probs(ss[g], ms[e] - float(i) * coff)
            l = l + lg
            o = o + pv(p, v_refs[g][0])
        m_s[step * n_sub + e] = ms[e]
        l_s[step * n_sub + e] = l
        o_s[step * n_sub + e] = o

    @pl.when(step == n_steps - 1)
    def _():
        n_ent = n_steps * n_sub + 1
        sn = scores(kn_ref[0], sc_ref[pl.ds(page_rows, rows_q), :])
        mn = jnp.maximum(colmax(sn), M_FLOOR)
        ln, pn = probs(sn, mn)
        m_s[n_ent - 1] = mn
        l_s[n_ent - 1] = ln
        o_s[n_ent - 1] = pv(pn, vn_ref[0])

        row = lax.broadcasted_iota(jnp.int32, (SUBLANES, LANES), 0)
        col = lax.broadcasted_iota(jnp.int32, (SUBLANES, LANES), 1)
        diag = row == col % SUBLANES
        pages_before = [e * sub for e in range(n_ent - 1)] + [n_steps * n_group]
        mt = [m_s[e] + float(pages_before[e]) * coff for e in range(n_ent)]
        mx = mt[0]
        for e in range(1, n_ent):
            mx = jnp.maximum(mx, mt[e])
        wts = [jnp.exp2(mt[e] - mx) for e in range(n_ent)]
        lsum = wts[0] * l_s[0]
        for e in range(1, n_ent):
            lsum = lsum + wts[e] * l_s[e]
        inv = 1.0 / _diag_rows(lsum, diag)
        wrows = [_diag_rows(wts[e], diag) * inv for e in range(n_ent)]
        wmat = jnp.concatenate(wrows + [jnp.zeros((LANES - n_ent, LANES), F32)], axis=0)
        wt = wmat.T
        out = wt[:, 0:1] * o_s[0]
        for e in range(1, n_ent):
            out = out + wt[:, e:e + 1] * o_s[e]
        half = LANES // 2
        lq1, lk1, lq2, lk2 = [vc_ref[pl.ds(r, 1), pl.ds(0, HEAD_DIM)] for r in range(4)]
        lam = _lambda_full(lq1, lk1, lq2, lk2, lam_init)
        od = out[:half] - lam * out[half:]
        o_ref[0] = _subln(od, vc_ref[pl.ds(4, 1), :], lam_init).astype(o_ref.dtype)


class _Decode:
    def __init__(self, q, k_new, v_new, cache_k, cache_v, slopes, lam_vecs, subln_g, lam_init, n_pages, b0, nb):
        db, t, nh, _ = q.shape
        n_phys, page, _, _ = cache_k.shape
        self.n_group = n_group = min(PAGES_PER_STEP, n_pages)
        self.n_sub = n_sub = min(ENTRIES_PER_STEP, n_group)
        assert nh == SUBLANES and 2 * t * nh == LANES and n_pages % n_group == 0 and n_group % n_sub == 0
        self.n_steps = n_steps = n_pages // n_group
        n_ent = n_steps * n_sub + 1
        assert n_ent <= LANES
        self.n_lin = nb * n_steps
        self.b0, self.nb, self.lam_init = b0, nb, lam_init
        rows_q = t * nh
        page_rows = page * nh
        ck = cache_k.reshape(n_phys, page_rows, V_DIM)
        cv = cache_v.reshape(n_phys, page_rows, V_DIM)
        q5 = q.reshape(db, t, nh, 2, HEAD_DIM)
        qt = jnp.einsum('bthcd,ce->bcdeth', q5, jnp.eye(2, dtype=q.dtype)).reshape(db, 2 * HEAD_DIM, LANES)
        kn = k_new.reshape(db, rows_q, V_DIM)
        vn = v_new.reshape(db, rows_q, V_DIM)
        sl2 = slopes * LOG2E
        col = jnp.arange(LANES)
        col_h, col_q = col % nh, (col // nh) % t
        rows = jnp.arange(page_rows)
        row_h, row_t = rows % nh, rows // nh
        bias = jnp.where(row_h[:, None] == col_h[None, :], (sl2[row_h] * row_t)[:, None], NEG).astype(F32)
        rn = jnp.arange(rows_q)
        rn_h, rn_t = rn % nh, rn // nh
        ok = (rn_h[:, None] == col_h[None, :]) & (rn_t[:, None] <= col_q[None, :])
        biasn = jnp.where(ok, (sl2[rn_h] * rn_t)[:, None], NEG).astype(F32)
        coff = jnp.broadcast_to((sl2[col_h] * page)[None, :], (SUBLANES, LANES)).astype(F32)
        sconst = jnp.concatenate([bias, biasn, coff], axis=0)
        pad = lambda v: jnp.pad(v.reshape(1, -1), ((0, 0), (0, V_DIM - v.size)))
        vconst = jnp.concatenate([pad(v) for v in lam_vecs] + [subln_g.reshape(1, V_DIM)]
                                 + [jnp.zeros((SUBLANES - 5, V_DIM), F32)], axis=0)
        self.page_rows, self.rows_q = page_rows, rows_q
        self.args = [ck] * n_group + [cv] * n_group + [qt.astype(BF16), kn, vn, sconst, vconst]
        self.block_shapes = ([(1, page_rows, V_DIM)] * (2 * n_group)
                             + [(1, 2 * HEAD_DIM, LANES), (1, rows_q, V_DIM), (1, rows_q, V_DIM),
                                sconst.shape, vconst.shape])
        self.out_block = (1, rows_q, V_DIM)
        self.out_shape = jax.ShapeDtypeStruct((nb, rows_q, V_DIM), BF16)
        self.scratch = [pltpu.VMEM((n_ent, SUBLANES, LANES), F32),
                        pltpu.VMEM((n_ent, SUBLANES, LANES), F32),
                        pltpu.VMEM((n_ent, LANES, V_DIM), F32)]

    def specs(self, lin_of):
        n_group, n_steps, b0 = self.n_group, self.n_steps, self.b0
        last = self.n_lin - 1

        def seq(*idx):
            lin = jnp.minimum(lin_of(*idx[:-1]), last)
            return lin // n_steps, lin % n_steps

        def page_map(g):
            def f(*idx):
                b, ps = seq(*idx)
                return (idx[-1][b0 + b, ps * n_group + g], 0, 0)
            return f

        def per_b(shape, off):
            def f(*idx):
                return (off + seq(*idx)[0],) + (0,) * (len(shape) - 1)
            return f

        const = lambda shape: (lambda *idx: (0,) * len(shape))
        in_specs = []
        for i, shape in enumerate(self.block_shapes):
            if i < 2 * n_group:
                in_specs.append(pl.BlockSpec(shape, page_map(i % n_group)))
            elif i < 2 * n_group + 3:
                in_specs.append(pl.BlockSpec(shape, per_b(shape, b0)))
            else:
                in_specs.append(pl.BlockSpec(shape, const(shape)))
        out_spec = pl.BlockSpec(self.out_block, per_b(self.out_block, 0))
        return in_specs, out_spec

    def run(self, lin, refs, n_host_steps, host):
        kw = dict(n_group=self.n_group, n_sub=self.n_sub, n_steps=self.n_steps, lam_init=self.lam_init)
        if n_host_steps == self.n_lin:
            _decode_step(lin % self.n_steps, refs, host, **kw)
        else:
            for piece in host:
                piece()

            @pl.when(lin < self.n_lin)
            def _():
                _decode_step(lin % self.n_steps, refs, [], **kw)


def _row_pieces(tm, piece_fn):
    rows = tm // HOST_PIECES
    return [functools.partial(piece_fn, pl.ds(r * rows, rows)) for r in range(HOST_PIECES)]


def _out_proj_kernel(att_ref, cv_ref, wa_ref, wc_ref, x_ref, o_ref):
    acc = jnp.dot(att_ref[...], wa_ref[...], preferred_element_type=F32)
    acc = acc + jnp.dot(cv_ref[...], wc_ref[...], preferred_element_type=F32)
    o_ref[...] = x_ref[...] + acc


def _out_proj(att, cv, w_out, x, tm, tn):
    m, half = att.shape
    d = x.shape[1]
    return pl.pallas_call(
        _out_proj_kernel,
        grid=(m // tm, d // tn),
        in_specs=[pl.BlockSpec((tm, half), lambda i, j: (i, 0)),
                  pl.BlockSpec((tm, half), lambda i, j: (i, 0)),
                  pl.BlockSpec((half, tn), lambda i, j: (0, j)),
                  pl.BlockSpec((half, tn), lambda i, j: (1, j)),
                  pl.BlockSpec((tm, tn), lambda i, j: (i, j))],
        out_specs=pl.BlockSpec((tm, tn), lambda i, j: (i, j)),
        out_shape=jax.ShapeDtypeStruct((m, d), F32),
        compiler_params=_params("parallel", "arbitrary"),
    )(att, cv, w_out, w_out, x)


def _up_kernel(pt_ref, h_ref, w_ref, *rest, dec, n_host_steps, cast_w):
    n_in = len(dec.block_shapes) if dec is not None else 0
    o_ref = rest[n_in + 1] if dec is not None else rest[0]
    if cast_w:
        rest, wb_ref = rest[:-1], rest[-1]
        wb_ref[...] = w_ref[...].astype(BF16)
    else:
        wb_ref = w_ref

    def piece(rows):
        acc = jnp.dot(h_ref[rows, :], wb_ref[...], preferred_element_type=F32)
        o_ref[rows, :] = _relu2(acc).astype(o_ref.dtype)

    if dec is None:
        piece(pl.ds(0, o_ref.shape[0]))
    else:
        refs = rest[:n_in] + (rest[n_in],) + rest[n_in + 2:]
        lin = pl.program_id(0) * pl.num_programs(1) + pl.program_id(1)
        dec.run(lin, refs, n_host_steps, _row_pieces(o_ref.shape[0], piece))


def _up_proj(h, w_up, page_table, dec, tm, tn):
    m, k = h.shape
    n = w_up.shape[1]
    grid = (m // tm, n // tn)
    in_specs = [pl.BlockSpec((tm, k), lambda i, j, pt: (i, 0), pipeline_mode=pl.Buffered(1 if dec is not None else 2)),
                pl.BlockSpec((k, tn), lambda i, j, pt: (0, j))]
    out_specs = [pl.BlockSpec((tm, tn), lambda i, j, pt: (i, j))]
    out_shape = [jax.ShapeDtypeStruct((m, n), BF16)]
    args, scratch = [h, w_up], []
    if dec is not None:
        assert grid[0] * grid[1] >= dec.n_lin
        d_in, d_out = dec.specs(lambda i, j: i * grid[1] + j)
        in_specs += d_in
        out_specs = [d_out] + out_specs
        out_shape = [dec.out_shape] + out_shape
        args += dec.args
        scratch = list(dec.scratch)
    cast_w = w_up.dtype != BF16
    if cast_w:
        scratch = scratch + [pltpu.VMEM((k, tn), BF16)]
    res = pl.pallas_call(
        functools.partial(_up_kernel, dec=dec, n_host_steps=grid[0] * grid[1], cast_w=cast_w),
        grid_spec=pltpu.PrefetchScalarGridSpec(num_scalar_prefetch=1, grid=grid, in_specs=in_specs,
                                               out_specs=out_specs, scratch_shapes=scratch),
        out_shape=out_shape,
        compiler_params=_params("arbitrary", "arbitrary", vmem_limit=VMEM_LIMIT_HOSTING),
    )(page_table, *args)
    return (res[1], res[0]) if dec is not None else (res[0], None)


def _down_kernel(pt_ref, a_ref, w_ref, x_ref, *rest, dec, n_host_steps):
    kk = pl.program_id(2)
    n_in = len(dec.block_shapes) if dec is not None else 0
    o_ref = rest[n_in + 1] if dec is not None else rest[0]

    @pl.when(kk == 0)
    def _():
        o_ref[...] = x_ref[...]

    def piece(rows):
        o_ref[rows, :] += jnp.dot(a_ref[rows, :], w_ref[...], preferred_element_type=F32)

    if dec is None:
        piece(pl.ds(0, o_ref.shape[0]))
    else:
        refs = rest[:n_in] + (rest[n_in],) + rest[n_in + 2:]
        lin = (pl.program_id(0) * pl.num_programs(1) + pl.program_id(1)) * pl.num_programs(2) + kk
        dec.run(lin, refs, n_host_steps, _row_pieces(o_ref.shape[0], piece))


def _down_proj(a, w_down, x, page_table, dec, tm, tn, tk):
    m, f = a.shape
    d = w_down.shape[1]
    grid = (m // tm, d // tn, f // tk)
    in_specs = [pl.BlockSpec((tm, tk), lambda i, j, k, pt: (i, k)),
                pl.BlockSpec((tk, tn), lambda i, j, k, pt: (k, j)),
                pl.BlockSpec((tm, tn), lambda i, j, k, pt: (i, j))]
    out_specs = [pl.BlockSpec((tm, tn), lambda i, j, k, pt: (i, j))]
    out_shape = [jax.ShapeDtypeStruct((m, d), F32)]
    args, scratch = [a, w_down, x], []
    n_host = grid[0] * grid[1] * grid[2]
    if dec is not None:
        assert n_host >= dec.n_lin
        d_in, d_out = dec.specs(lambda i, j, k: (i * grid[1] + j) * grid[2] + k)
        in_specs += d_in
        out_specs = [d_out] + out_specs
        out_shape = [dec.out_shape] + out_shape
        args += dec.args
        scratch = dec.scratch
    res = pl.pallas_call(
        functools.partial(_down_kernel, dec=dec, n_host_steps=n_host),
        grid_spec=pltpu.PrefetchScalarGridSpec(num_scalar_prefetch=1, grid=grid, in_specs=in_specs,
                                               out_specs=out_specs, scratch_shapes=scratch),
        out_shape=out_shape,
        compiler_params=_params("arbitrary", "arbitrary", "arbitrary"),
    )(page_table, *args)
    return (res[1], res[0]) if dec is not None else (res[0], None)


def _layer(xp, xs, ck, cvv, st, page_table, w, layer, slopes):
    bsz, s, d = xp.shape
    db, t, _ = xs.shape
    mp, ms = bsz * s, db * t
    att_w = d // 2
    conv_ch = d - att_w
    nh = att_w // V_DIM
    hist = CONV_WIDTH - 1
    lam_init = 0.8 - 0.6 * math.exp(-0.3 * layer)
    lam_vecs, sg = w['lam_vecs'], w['subln_g']
    n_pages = page_table.shape[1]
    tmp, tms = _tile(mp, 1024), _tile(ms, 1024)
    assert tms == ms

    xs2 = xs.reshape(ms, d)
    hs = _rms_cast(xs2, w['attn_norm_g'], _tile(ms, 256))
    offs = (0, att_w, 2 * att_w, 3 * att_w, 3 * att_w + conv_ch)
    (qs, ks_f, _, vs_f, _, us), w_in_pieces = _in_proj(hs, w, [(w['w_in'], o) for o in offs], tms, 512, True)
    r4 = lambda a: a.reshape(db, t, nh, V_DIM)
    nb_a = db // 2
    mk_dec = lambda b0, nb: _Decode(r4(qs), r4(ks_f), r4(vs_f), ck, cvv, slopes, lam_vecs, sg, lam_init,
                                    n_pages, b0, nb)
    w_out_b, w_down_b = _cast_bf16(w['w_out']), _cast_bf16(w['w_down'])

    xp2 = xp.reshape(mp, d)
    hp = _rms_cast(xp2, w['attn_norm_g'], _tile(mp, 256))
    (qp, kp_f, kp_b, vp_f, vp_b, upr), _ = _in_proj(hp, w, [(p, 0) for p in w_in_pieces], tmp, 1024, False)
    tq = _tile(s, 512)
    r3 = lambda a: a.reshape(bsz, s, att_w)
    att_p = _prompt_attention(r3(qp), r3(kp_b), r3(vp_b), slopes, lam_vecs, sg, lam_init, tq).reshape(mp, att_w)
    up3 = upr.reshape(bsz, s, conv_ch)
    conv_tt = _tile(s, 128)
    cv_p = _conv_module(up3, up3, conv_tt // HALO, True, w, conv_tt).reshape(mp, conv_ch)
    x1p = _out_proj(att_p, cv_p, w_out_b, xp2, tmp, 1024)
    h2p = _rms_cast(x1p, w['mlp_norm_g'], _tile(mp, 256))
    act_p, att_a = _up_proj(h2p, w['w_up'], page_table, mk_dec(0, nb_a), tmp, 512)
    yp, att_b = _down_proj(act_p, w_down_b, x1p, page_table, mk_dec(nb_a, db - nb_a), tmp, 1024,
                           _tile(w_down_b.shape[0], 2048))

    att_s = jnp.concatenate([att_a, att_b], axis=0).reshape(ms, att_w)
    us3 = us.reshape(db, t, conv_ch)
    halo_s = jnp.pad(st, ((0, 0), (HALO - hist, 0), (0, 0)))
    cv_s = _conv_module(us3, halo_s, 0, False, w, t).reshape(ms, conv_ch)
    x1s = _out_proj(att_s, cv_s, w_out_b, xs2, tms, 1024)
    h2s = _rms_cast(x1s, w['mlp_norm_g'], _tile(ms, 256))
    act_s, _ = _up_proj(h2s, w['w_up'], page_table, None, tms, 512)
    ys, _ = _down_proj(act_s, w_down_b, x1s, page_table, None, tms, 1024, _tile(w_down_b.shape[0], 4096))

    r5 = lambda a, b_, t_: a.reshape(b_, t_, nh, V_DIM)
    cs = jnp.concatenate([st, us3], axis=1)[:, -hist:]
    return (yp.reshape(bsz, s, d), ys.reshape(db, t, d),
            (r5(kp_f, bsz, s), r5(vp_f, bsz, s), up3[:, -hist:], r5(ks_f, db, t), r5(vs_f, db, t), cs))


def kernel(x_prompt, x_sample, cache_k, cache_v, state_conv, page_table, attn_norm_g, w_in, q_norm_g, k_norm_g, lambda_q1, lambda_k1, lambda_q2, lambda_k2, subln_g, conv_w, conv_b, conv_ln_g, conv_ln_b, w_out, mlp_norm_g, w_up, w_down):
    depth = w_in.shape[0]
    d = x_prompt.shape[-1]
    att_w = d // 2
    nh = att_w // V_DIM
    slopes = jnp.exp2(-8.0 * jnp.arange(1, nh + 1, dtype=F32) / nh)
    xp, xs = x_prompt, x_sample
    outs = [[] for _ in range(6)]
    for l in range(depth):
        w = {'attn_norm_g': attn_norm_g[l], 'w_in': w_in[l],
             'q_gain': jnp.tile(q_norm_g[l], att_w // HEAD_DIM).reshape(1, att_w),
             'k_gain': jnp.tile(k_norm_g[l], att_w // HEAD_DIM).reshape(1, att_w),
             'lam_vecs': [a[l].reshape(1, HEAD_DIM) for a in (lambda_q1, lambda_k1, lambda_q2, lambda_k2)],
             'subln_g': subln_g[l], 'conv_w': conv_w[l], 'conv_b': conv_b[l], 'conv_ln_g': conv_ln_g[l],
             'conv_ln_b': conv_ln_b[l], 'w_out': w_out[l], 'mlp_norm_g': mlp_norm_g[l], 'w_up': w_up[l],
             'w_down': w_down[l]}
        xp, xs, extra = _layer(xp, xs, cache_k[l], cache_v[l], state_conv[l], page_table, w, l, slopes)
        for lst, val in zip(outs, extra):
            lst.append(val)
    return (xp, xs) + tuple(jnp.stack(o) for o in outs)
```

```python
import functools
import math

import jax
import jax.numpy as jnp
from jax import lax
from jax.experimental import pallas as pl
from jax.experimental.pallas import tpu as pltpu

HEAD_DIM = 128
V_DIM = 2 * HEAD_DIM
CONV_WIDTH = 31
EPS = 1e-6
SCALE = HEAD_DIM ** -0.5
LOG2E = math.log2(math.e)
NEG = -1e30
M_FLOOR = -1e20
LANES = 128
SUBLANES = 8
VMEM_LIMIT = 56 * 1024 * 1024
VMEM_LIMIT_HOSTING = 62 * 1024 * 1024
HALO = 32
PAGES_PER_STEP = 4
ENTRIES_PER_STEP = 2
CAST_BLOCK_BYTES = 8 * 1024 * 1024
HOST_PIECES = 4
BF16 = jnp.bfloat16
F32 = jnp.float32


def _params(*sem, vmem_limit=VMEM_LIMIT):
    return pltpu.CompilerParams(dimension_semantics=sem, vmem_limit_bytes=vmem_limit)


def _tile(n, pref):
    t = min(n, pref)
    assert n % t == 0
    return t


def _rms_cast_kernel(x_ref, g_ref, o_ref):
    x = x_ref[...]
    ms = jnp.mean(x * x, axis=-1, keepdims=True)
    o_ref[...] = ((x * lax.rsqrt(ms + EPS)) * g_ref[...]).astype(o_ref.dtype)


def _rms_cast(x, g, tm):
    m, d = x.shape
    return pl.pallas_call(
        _rms_cast_kernel,
        grid=(m // tm,),
        in_specs=[pl.BlockSpec((tm, d), lambda i: (i, 0)),
                  pl.BlockSpec((1, d), lambda i: (0, 0))],
        out_specs=pl.BlockSpec((tm, d), lambda i: (i, 0)),
        out_shape=jax.ShapeDtypeStruct((m, d), BF16),
        compiler_params=_params("parallel"),
    )(x, g.reshape(1, d))


def _cast_kernel(w_ref, o_ref):
    o_ref[...] = w_ref[...].astype(o_ref.dtype)


def _cast_bf16(w):
    r, c = w.shape
    tr = _tile(r, max(SUBLANES, CAST_BLOCK_BYTES // (4 * c)))
    return pl.pallas_call(
        _cast_kernel,
        grid=(r // tr,),
        in_specs=[pl.BlockSpec((tr, c), lambda i: (i, 0))],
        out_specs=pl.BlockSpec((tr, c), lambda i: (i, 0)),
        out_shape=jax.ShapeDtypeStruct((r, c), BF16),
        compiler_params=_params("parallel"),
    )(w)


def _group_rms(acc, g):
    outs = []
    for c in range(acc.shape[1] // HEAD_DIM):
        xg = acc[:, c * HEAD_DIM:(c + 1) * HEAD_DIM]
        ms = jnp.mean(xg * xg, axis=-1, keepdims=True)
        outs.append(xg * lax.rsqrt(ms + EPS))
    return jnp.concatenate(outs, axis=-1) * g


def _relu2(acc):
    r = jnp.maximum(acc, 0.0)
    return r * r


def _proj_kernel(*refs, n_w, n_out, epilogue, cast_w):
    h_ref, w_refs = refs[0], refs[1:1 + n_w]
    pos = 1 + n_w
    g_ref = None
    if epilogue == "norm":
        g_ref, pos = refs[pos], pos + 1
    o_refs = refs[pos:pos + n_out]
    wb_refs = refs[pos + n_out:]
    h = h_ref[...]
    accs = []
    for i, w_ref in enumerate(w_refs):
        w = w_ref[...]
        if cast_w:
            w = w.astype(BF16)
            wb_refs[i][...] = w
        accs.append(jnp.dot(h, w, preferred_element_type=F32))
    if epilogue == "norm":
        y = _group_rms(accs[0], g_ref[...])
    elif epilogue == "glu":
        y = accs[0] * jax.nn.sigmoid(accs[1])
    elif epilogue == "relu2":
        y = _relu2(accs[0])
    else:
        y = accs[0]
    for o_ref in o_refs:
        o_ref[...] = y.astype(o_ref.dtype)


def _proj(h, ws, n_cols, epilogue, out_dtypes, tm, tn, gain=None, cast_w=False):
    m, k = h.shape
    in_specs = [pl.BlockSpec((tm, k), lambda i, j: (i, 0))]
    args = [h]
    for arr, off in ws:
        in_specs.append(pl.BlockSpec((k, tn), lambda i, j, o=off // tn: (0, o + j)))
        args.append(arr)
    if gain is not None:
        in_specs.append(pl.BlockSpec((1, tn), lambda i, j: (0, j)))
        args.append(gain)
    out_specs = [pl.BlockSpec((tm, tn), lambda i, j: (i, j)) for _ in out_dtypes]
    out_shape = [jax.ShapeDtypeStruct((m, n_cols), dt) for dt in out_dtypes]
    if cast_w:
        assert m == tm
        out_specs += [pl.BlockSpec((k, tn), lambda i, j: (0, j)) for _ in ws]
        out_shape += [jax.ShapeDtypeStruct((k, n_cols), BF16) for _ in ws]
    kern = functools.partial(_proj_kernel, n_w=len(ws), n_out=len(out_dtypes), epilogue=epilogue, cast_w=cast_w)
    return pl.pallas_call(
        kern, grid=(m // tm, n_cols // tn), in_specs=in_specs, out_specs=out_specs, out_shape=out_shape,
        compiler_params=_params("parallel", "arbitrary"),
    )(*args)


def _in_proj(h, w, ws, tm, tn, cast_w):
    wq, wk, wv, wa, wg = ws
    qk_w = w['q_gain'].shape[1]
    res_q = _proj(h, [wq], qk_w, "norm", [BF16], tm, tn, gain=w['q_gain'], cast_w=cast_w)
    res_k = _proj(h, [wk], qk_w, "norm", [F32, BF16], tm, tn, gain=w['k_gain'], cast_w=cast_w)
    res_v = _proj(h, [wv], qk_w, "plain", [F32, BF16], tm, tn, cast_w=cast_w)
    res_u = _proj(h, [wa, wg], qk_w, "glu", [F32], tm, tn // 2, cast_w=cast_w)
    acts = (res_q[0], res_k[0], res_k[1], res_v[0], res_v[1], res_u[0])
    pieces = (res_q[1], res_k[2], res_v[2], res_u[1], res_u[2]) if cast_w else None
    return acts, pieces


def _conv_kernel(halo_ref, u_ref, w_ref, b_ref, lg_ref, lb_ref, o_ref, ext_s, sh_s, y_s,
                 *, zero_first, cchunk):
    tt, ch = u_ref.shape[1], u_ref.shape[2]
    ext_s[pl.ds(0, HALO), :] = halo_ref[0]
    if zero_first:
        @pl.when(pl.program_id(1) == 0)
        def _():
            ext_s[pl.ds(0, HALO), :] = jnp.zeros((HALO, ch), F32)
    ext_s[pl.ds(HALO, tt), :] = u_ref[0]
    first = HALO - (CONV_WIDTH - 1)
    span = sh_s.shape[1]

    def chunk(c, carry):
        c0 = pl.multiple_of(c * cchunk, cchunk)
        for r in range(1, SUBLANES):
            sh_s[r] = ext_s[pl.ds(r, span), pl.ds(c0, cchunk)]
        acc = jnp.broadcast_to(b_ref[:, pl.ds(c0, cchunk)], (tt, cchunk))
        for tap in range(CONV_WIDTH):
            r, base = (first + tap) % SUBLANES, (first + tap) // SUBLANES * SUBLANES
            src = ext_s[pl.ds(base, tt), pl.ds(c0, cchunk)] if r == 0 else sh_s[r, pl.ds(base, tt), :]
            acc = acc + src * w_ref[pl.ds(tap, 1), pl.ds(c0, cchunk)]
        y_s[:, pl.ds(c0, cchunk)] = acc
        return carry

    lax.fori_loop(0, ch // cchunk, chunk, 0)
    y = y_s[...]
    mu = jnp.mean(y, axis=-1, keepdims=True)
    d = y - mu
    var = jnp.mean(d * d, axis=-1, keepdims=True)
    yn = d * lax.rsqrt(var + EPS) * lg_ref[...] + lb_ref[...]
    o_ref[0] = (yn * jax.nn.sigmoid(yn)).astype(o_ref.dtype)


def _conv_module(u, halo, halo_blocks_per_tile, zero_first, w, tt):
    bsz, t, ch = u.shape
    hb = halo_blocks_per_tile
    cchunk = 2 * LANES
    kern = functools.partial(_conv_kernel, zero_first=zero_first, cchunk=cchunk)
    row = lambda: pl.BlockSpec((1, ch), lambda b, i: (0, 0))
    return pl.pallas_call(
        kern,
        grid=(bsz, t // tt),
        in_specs=[pl.BlockSpec((1, HALO, ch), lambda b, i: (b, jnp.maximum(i * hb - 1, 0), 0)),
                  pl.BlockSpec((1, tt, ch), lambda b, i: (b, i, 0)),
                  pl.BlockSpec((CONV_WIDTH, ch), lambda b, i: (0, 0)),
                  row(), row(), row()],
        out_specs=pl.BlockSpec((1, tt, ch), lambda b, i: (b, i, 0)),
        out_shape=jax.ShapeDtypeStruct((bsz, t, ch), BF16),
        scratch_shapes=[pltpu.VMEM((HALO + tt, ch), F32),
                        pltpu.VMEM((SUBLANES, HALO - SUBLANES + tt, cchunk), F32),
                        pltpu.VMEM((tt, ch), F32)],
        compiler_params=_params("parallel", "arbitrary"),
    )(halo, u, w['conv_w'], w['conv_b'].reshape(1, ch), w['conv_ln_g'].reshape(1, ch), w['conv_ln_b'].reshape(1, ch))


def _lambda_full(lq1, lk1, lq2, lk2, lam_init):
    return (jnp.exp(jnp.sum(lq1 * lk1, axis=-1, keepdims=True))
            - jnp.exp(jnp.sum(lq2 * lk2, axis=-1, keepdims=True)) + lam_init)


def _subln(o, g, lam_init):
    ms = jnp.mean(o * o, axis=-1, keepdims=True)
    return (o * lax.rsqrt(ms + EPS)) * g * (1.0 - lam_init)


def _prompt_attn_kernel(slope_ref, q_ref, k_ref, v_ref, bias_ref,
                        lq1_ref, lk1_ref, lq2_ref, lk2_ref, sg_ref, o_ref,
                        vt_s, m_s, l_s, a_s, acc_s, p_s, *, lam_init):
    h = pl.program_id(1)
    qi = pl.program_id(2)
    tq = q_ref.shape[1]
    tile_off = slope_ref[h] * (LOG2E * tq)

    @pl.when(qi == 0)
    def _():
        for j in range(vt_s.shape[0]):
            vt_s[j] = v_ref[0, pl.ds(j * tq, tq), :].astype(F32).T.astype(vt_s.dtype)

    q = q_ref[0]
    m_s[...] = jnp.full(m_s.shape, NEG, F32)
    l_s[...] = jnp.zeros(l_s.shape, F32)
    acc_s[...] = jnp.zeros(acc_s.shape, F32)

    def softmax_tile(j):
        r0 = pl.multiple_of(j * tq, tq)
        k = k_ref[0, pl.ds(r0, tq), :]
        bias = bias_ref[0, jnp.where(j == qi, 1, 0)]
        offs = -tile_off * (qi - j).astype(F32)
        ps, alphas = [], []
        for c in range(2):
            cols = pl.ds(c * tq, tq)
            t = lax.dot_general(k[:, c * HEAD_DIM:(c + 1) * HEAD_DIM], q[:, c * HEAD_DIM:(c + 1) * HEAD_DIM],
                                (((1,), (1,)), ((), ())), preferred_element_type=F32) * (SCALE * LOG2E) + bias
            m_prev = m_s[:, cols]
            m_new = jnp.maximum(m_prev, jnp.max(t, axis=0, keepdims=True) + offs)
            p = jnp.exp2(t - (m_new - offs))
            alpha = jnp.exp2(m_prev - m_new)
            l_s[:, cols] = alpha * l_s[:, cols] + jnp.sum(p, axis=0, keepdims=True)
            m_s[:, cols] = m_new
            ps.append(p.astype(p_s.dtype))
            alphas.append(alpha)
        return ps, alphas

    def stage(ps, alphas):
        for c in range(2):
            p_s[:, pl.ds(c * tq, tq)] = ps[c]
            a_s[:, pl.ds(c * tq, tq)] = alphas[c]

    def accumulate(j):
        acc_s[...] = a_s[...] * acc_s[...] + jnp.dot(vt_s[j], p_s[...], preferred_element_type=F32)

    stage(*softmax_tile(0))

    def body(j, carry):
        ps, alphas = softmax_tile(j)
        accumulate(j - 1)
        stage(ps, alphas)
        return carry

    lax.fori_loop(1, qi + 1, body, 0)
    accumulate(qi)

    lam = _lambda_full(lq1_ref[...], lk1_ref[...], lq2_ref[...], lk2_ref[...], lam_init)
    inv = 1.0 / l_s[...]
    ot = acc_s[:, pl.ds(0, tq)] * inv[:, :tq] - lam * (acc_s[:, pl.ds(tq, tq)] * inv[:, tq:])
    o_ref[0] = _subln(ot.T, sg_ref[...], lam_init).astype(o_ref.dtype)


def _prompt_attention(q, k, v, slopes, lam_vecs, subln_g, lam_init, tq):
    bsz, s, width = q.shape
    nh = width // V_DIM
    rel = (jnp.arange(tq, dtype=jnp.int32)[None, :] - jnp.arange(tq, dtype=jnp.int32)[:, None])
    b_open = (-LOG2E) * slopes[:, None, None] * rel.astype(F32)[None]
    bias = jnp.stack([b_open, jnp.where(rel[None] >= 0, b_open, NEG)], axis=1)
    vec = lambda n: pl.BlockSpec((1, n), lambda b, h, i, sl: (0, 0))
    grid_spec = pltpu.PrefetchScalarGridSpec(
        num_scalar_prefetch=1,
        grid=(bsz, nh, s // tq),
        in_specs=[pl.BlockSpec((1, tq, V_DIM), lambda b, h, i, sl: (b, i, h)),
                  pl.BlockSpec((1, s, V_DIM), lambda b, h, i, sl: (b, 0, h)),
                  pl.BlockSpec((1, s, V_DIM), lambda b, h, i, sl: (b, 0, h)),
                  pl.BlockSpec((1, 2, tq, tq), lambda b, h, i, sl: (h, 0, 0, 0)),
                  vec(HEAD_DIM), vec(HEAD_DIM), vec(HEAD_DIM), vec(HEAD_DIM), vec(V_DIM)],
        out_specs=pl.BlockSpec((1, tq, V_DIM), lambda b, h, i, sl: (b, i, h)),
        scratch_shapes=[pltpu.VMEM((s // tq, V_DIM, tq), BF16),
                        pltpu.VMEM((1, 2 * tq), F32),
                        pltpu.VMEM((1, 2 * tq), F32),
                        pltpu.VMEM((1, 2 * tq), F32),
                        pltpu.VMEM((V_DIM, 2 * tq), F32),
                        pltpu.VMEM((tq, 2 * tq), BF16)],
    )
    return pl.pallas_call(
        functools.partial(_prompt_attn_kernel, lam_init=lam_init),
        grid_spec=grid_spec,
        out_shape=jax.ShapeDtypeStruct((bsz, s, width), BF16),
        compiler_params=_params("parallel", "parallel", "arbitrary"),
    )(slopes, q, k, v, bias, *lam_vecs, subln_g.reshape(1, V_DIM))


def _diag_rows(x, diag):
    return jnp.sum(jnp.where(diag, x, 0.0), axis=0, keepdims=True)


def _decode_step(step, refs, host, *, n_group, n_sub, n_steps, lam_init):
    k_refs = refs[:n_group]
    v_refs = refs[n_group:2 * n_group]
    qt_ref, kn_ref, vn_ref, sc_ref, vc_ref, o_ref, m_s, l_s, o_s = refs[2 * n_group:]
    page_rows, rows_q = k_refs[0].shape[1], kn_ref.shape[1]
    qt = qt_ref[0]
    coff = sc_ref[pl.ds(page_rows + rows_q, SUBLANES), :]

    def scores(kf, bias):
        s = jnp.dot(kf.astype(BF16), qt, preferred_element_type=F32)
        return s * (SCALE * LOG2E) + bias

    def colmax(s):
        return jnp.max(s.reshape(-1, SUBLANES, LANES), axis=0)

    def probs(s, m):
        n = s.shape[0] // SUBLANES
        p = jnp.exp2(s.reshape(n, SUBLANES, LANES) - m[None])
        return jnp.sum(p, axis=0), p.reshape(s.shape).astype(BF16)

    def pv(p, vf):
        return lax.dot_general(p, vf.astype(BF16), (((0,), (0,)), ((), ())),
                               preferred_element_type=F32)

    n_slots = 2 * n_group
    slot_of = [i * n_slots // max(len(host), 1) for i in range(len(host))]

    def host_work(slot):
        for i, piece in enumerate(host):
            if slot_of[i] == slot:
                piece()

    sub = n_group // n_sub
    bias = sc_ref[pl.ds(0, page_rows), :]
    ss = []
    for g in range(n_group):
        host_work(g)
        ss.append(scores(k_refs[g][0], bias))
    ms = []
    for e in range(n_sub):
        m = jnp.full((SUBLANES, LANES), M_FLOOR, F32)
        for i in range(sub):
            m = jnp.maximum(m, colmax(ss[e * sub + i]) + float(i) * coff)
        ms.append(m)
    for e in range(n_sub):
        l = jnp.zeros((SUBLANES, LANES), F32)
        o = jnp.zeros((LANES, V_DIM), F32)
        for i in range(sub):
            g = e * sub + i
            host_work(n_group + g)
            lg, p = probs(ss[g], ms[e] - float(i) * coff)
            l = l + lg
            o = o + pv(p, v_refs[g][0])
        m_s[step * n_sub + e] = ms[e]
        l_s[step * n_sub + e] = l
        o_s[step * n_sub + e] = o

    @pl.when(step == n_steps - 1)
    def _():
        n_ent = n_steps * n_sub + 1
        sn = scores(kn_ref[0], sc_ref[pl.ds(page_rows, rows_q), :])
        mn = jnp.maximum(colmax(sn), M_FLOOR)
        ln, pn = probs(sn, mn)
        m_s[n_ent - 1] = mn
        l_s[n_ent - 1] = ln
        o_s[n_ent - 1] = pv(pn, vn_ref[0])

        row = lax.broadcasted_iota(jnp.int32, (SUBLANES, LANES), 0)
        col = lax.broadcasted_iota(jnp.int32, (SUBLANES, LANES), 1)
        diag = row == col % SUBLANES
        pages_before = [e * sub for e in range(n_ent - 1)] + [n_steps * n_group]
        mt = [m_s[e] + float(pages_before[e]) * coff for e in range(n_ent)]
        mx = mt[0]
        for e in range(1, n_ent):
            mx = jnp.maximum(mx, mt[e])
        wts = [jnp.exp2(mt[e] - mx) for e in range(n_ent)]
        lsum = wts[0] * l_s[0]
        for e in range(1, n_ent):
            lsum = lsum + wts[e] * l_s[e]
        inv = 1.0 / _diag_rows(lsum, diag)
        wrows = [_diag_rows(wts[e], diag) * inv for e in range(n_ent)]
        wmat = jnp.concatenate(wrows + [jnp.zeros((LANES - n_ent, LANES), F32)], axis=0)
        wt = wmat.T
        out = wt[:, 0:1] * o_s[0]
        for e in range(1, n_ent):
            out = out + wt[:, e:e + 1] * o_s[e]
        half = LANES // 2
        lq1, lk1, lq2, lk2 = [vc_ref[pl.ds(r, 1), pl.ds(0, HEAD_DIM)] for r in range(4)]
        lam = _lambda_full(lq1, lk1, lq2, lk2, lam_init)
        od = out[:half] - lam * out[half:]
        o_ref[0] = _subln(od, vc_ref[pl.ds(4, 1), :], lam_init).astype(o_ref.dtype)


class _Decode:
    def __init__(self, q, k_new, v_new, cache_k, cache_v, slopes, lam_vecs, subln_g, lam_init, n_pages, b0, nb):
        db, t, nh, _ = q.shape
        n_phys, page, _, _ = cache_k.shape
        self.n_group = n_group = min(PAGES_PER_STEP, n_pages)
        self.n_sub = n_sub = min(ENTRIES_PER_STEP, n_group)
        assert nh == SUBLANES and 2 * t * nh == LANES and n_pages % n_group == 0 and n_group % n_sub == 0
        self.n_steps = n_steps = n_pages // n_group
        n_ent = n_steps * n_sub + 1
        assert n_ent <= LANES
        self.n_lin = nb * n_steps
        self.b0, self.nb, self.lam_init = b0, nb, lam_init
        rows_q = t * nh
        page_rows = page * nh
        ck = cache_k.reshape(n_phys, page_rows, V_DIM)
        cv = cache_v.reshape(n_phys, page_rows, V_DIM)
        q5 = q.reshape(db, t, nh, 2, HEAD_DIM)
        qt = jnp.einsum('bthcd,ce->bcdeth', q5, jnp.eye(2, dtype=q.dtype)).reshape(db, 2 * HEAD_DIM, LANES)
        kn = k_new.reshape(db, rows_q, V_DIM)
        vn = v_new.reshape(db, rows_q, V_DIM)
        sl2 = slopes * LOG2E
        col = jnp.arange(LANES)
        col_h, col_q = col % nh, (col // nh) % t
        rows = jnp.arange(page_rows)
        row_h, row_t = rows % nh, rows // nh
        bias = jnp.where(row_h[:, None] == col_h[None, :], (sl2[row_h] * row_t)[:, None], NEG).astype(F32)
        rn = jnp.arange(rows_q)
        rn_h, rn_t = rn % nh, rn // nh
        ok = (rn_h[:, None] == col_h[None, :]) & (rn_t[:, None] <= col_q[None, :])
        biasn = jnp.where(ok, (sl2[rn_h] * rn_t)[:, None], NEG).astype(F32)
        coff = jnp.broadcast_to((sl2[col_h] * page)[None, :], (SUBLANES, LANES)).astype(F32)
        sconst = jnp.concatenate([bias, biasn, coff], axis=0)
        pad = lambda v: jnp.pad(v.reshape(1, -1), ((0, 0), (0, V_DIM - v.size)))
        vconst = jnp.concatenate([pad(v) for v in lam_vecs] + [subln_g.reshape(1, V_DIM)]
                                 + [jnp.zeros((SUBLANES - 5, V_DIM), F32)], axis=0)
        self.page_rows, self.rows_q = page_rows, rows_q
        self.args = [ck] * n_group + [cv] * n_group + [qt.astype(BF16), kn, vn, sconst, vconst]
        self.block_shapes = ([(1, page_rows, V_DIM)] * (2 * n_group)
                             + [(1, 2 * HEAD_DIM, LANES), (1, rows_q, V_DIM), (1, rows_q, V_DIM),
                                sconst.shape, vconst.shape])
        self.out_block = (1, rows_q, V_DIM)
        self.out_shape = jax.ShapeDtypeStruct((nb, rows_q, V_DIM), BF16)
        self.scratch = [pltpu.VMEM((n_ent, SUBLANES, LANES), F32),
                        pltpu.VMEM((n_ent, SUBLANES, LANES), F32),
                        pltpu.VMEM((n_ent, LANES, V_DIM), F32)]

    def specs(self, lin_of):
        n_group, n_steps, b0 = self.n_group, self.n_steps, self.b0
        last = self.n_lin - 1

        def seq(*idx):
            lin = jnp.minimum(lin_of(*idx[:-1]), last)
            return lin // n_steps, lin % n_steps

        def page_map(g):
            def f(*idx):
                b, ps = seq(*idx)
                return (idx[-1][b0 + b, ps * n_group + g], 0, 0)
            return f

        def per_b(shape, off):
            def f(*idx):
                return (off + seq(*idx)[0],) + (0,) * (len(shape) - 1)
            return f

        const = lambda shape: (lambda *idx: (0,) * len(shape))
        in_specs = []
        for i, shape in enumerate(self.block_shapes):
            if i < 2 * n_group:
                in_specs.append(pl.BlockSpec(shape, page_map(i % n_group)))
            elif i < 2 * n_group + 3:
                in_specs.append(pl.BlockSpec(shape, per_b(shape, b0)))
            else:
                in_specs.append(pl.BlockSpec(shape, const(shape)))
        out_spec = pl.BlockSpec(self.out_block, per_b(self.out_block, 0))
        return in_specs, out_spec

    def run(self, lin, refs, n_host_steps, host):
        kw = dict(n_group=self.n_group, n_sub=self.n_sub, n_steps=self.n_steps, lam_init=self.lam_init)
        if n_host_steps == self.n_lin:
            _decode_step(lin % self.n_steps, refs, host, **kw)
        else:
            for piece in host:
                piece()

            @pl.when(lin < self.n_lin)
            def _():
                _decode_step(lin % self.n_steps, refs, [], **kw)


def _row_pieces(tm, piece_fn):
    rows = tm // HOST_PIECES
    return [functools.partial(piece_fn, pl.ds(r * rows, rows)) for r in range(HOST_PIECES)]


def _out_proj_kernel(att_ref, cv_ref, wa_ref, wc_ref, x_ref, g_ref, o_ref, xg_ref, ssq_ref):
    acc = jnp.dot(att_ref[...], wa_ref[...], preferred_element_type=F32)
    acc = acc + jnp.dot(cv_ref[...], wc_ref[...], preferred_element_type=F32)
    x1 = x_ref[...] + acc
    o_ref[...] = x1
    xg_ref[...] = (x1 * g_ref[...]).astype(xg_ref.dtype)
    sq = x1 * x1
    part = sq[:, :LANES]
    for c in range(1, sq.shape[1] // LANES):
        part = part + sq[:, c * LANES:(c + 1) * LANES]
    ssq_ref[...] = part


def _out_proj(att, cv, w_out, x, gain, tm, tn):
    m, half = att.shape
    d = x.shape[1]
    return pl.pallas_call(
        _out_proj_kernel,
        grid=(m // tm, d // tn),
        in_specs=[pl.BlockSpec((tm, half), lambda i, j: (i, 0), pipeline_mode=pl.Buffered(1)),
                  pl.BlockSpec((tm, half), lambda i, j: (i, 0), pipeline_mode=pl.Buffered(1)),
                  pl.BlockSpec((half, tn), lambda i, j: (0, j)),
                  pl.BlockSpec((half, tn), lambda i, j: (1, j)),
                  pl.BlockSpec((tm, tn), lambda i, j: (i, j)),
                  pl.BlockSpec((1, tn), lambda i, j: (0, j))],
        out_specs=[pl.BlockSpec((tm, tn), lambda i, j: (i, j)),
                   pl.BlockSpec((tm, tn), lambda i, j: (i, j)),
                   pl.BlockSpec((tm, LANES), lambda i, j: (i, j))],
        out_shape=[jax.ShapeDtypeStruct((m, d), F32), jax.ShapeDtypeStruct((m, d), BF16),
                   jax.ShapeDtypeStruct((m, d // tn * LANES), F32)],
        compiler_params=_params("parallel", "arbitrary"),
    )(att, cv, w_out, w_out, x, gain.reshape(1, d))


def _up_kernel(pt_ref, h_ref, w_ref, ssq_ref, *rest, dec, n_host_steps, cast_w):
    n_in = len(dec.block_shapes) if dec is not None else 0
    o_ref = rest[n_in + 1] if dec is not None else rest[0]
    rest, r2_ref = rest[:-1], rest[-1]
    if cast_w:
        rest, wb_ref = rest[:-1], rest[-1]
        wb_ref[...] = w_ref[...].astype(BF16)
    else:
        wb_ref = w_ref
    ms = jnp.sum(ssq_ref[...], axis=-1, keepdims=True) * (1.0 / h_ref.shape[1])
    r2_ref[...] = 1.0 / (ms + EPS)

    def piece(rows):
        acc = jnp.dot(h_ref[rows, :], wb_ref[...], preferred_element_type=F32)
        o_ref[rows, :] = (_relu2(acc) * r2_ref[rows, :]).astype(o_ref.dtype)

    if dec is None:
        piece(pl.ds(0, o_ref.shape[0]))
    else:
        refs = rest[:n_in] + (rest[n_in],) + rest[n_in + 2:]
        lin = pl.program_id(0) * pl.num_programs(1) + pl.program_id(1)
        dec.run(lin, refs, n_host_steps, _row_pieces(o_ref.shape[0], piece))


def _up_proj(h, ssq, w_up, page_table, dec, tm, tn):
    m, k = h.shape
    n = w_up.shape[1]
    grid = (m // tm, n // tn)
    once = pl.Buffered(1 if dec is not None else 2)
    in_specs = [pl.BlockSpec((tm, k), lambda i, j, pt: (i, 0), pipeline_mode=once),
                pl.BlockSpec((k, tn), lambda i, j, pt: (0, j)),
                pl.BlockSpec((tm, ssq.shape[1]), lambda i, j, pt: (i, 0), pipeline_mode=once)]
    out_specs = [pl.BlockSpec((tm, tn), lambda i, j, pt: (i, j))]
    out_shape = [jax.ShapeDtypeStruct((m, n), BF16)]
    args, scratch = [h, w_up, ssq], []
    if dec is not None:
        assert grid[0] * grid[1] >= dec.n_lin
        d_in, d_out = dec.specs(lambda i, j: i * grid[1] + j)
        in_specs += d_in
        out_specs = [d_out] + out_specs
        out_shape = [dec.out_shape] + out_shape
        args += dec.args
        scratch = list(dec.scratch)
    cast_w = w_up.dtype != BF16
    if cast_w:
        scratch = scratch + [pltpu.VMEM((k, tn), BF16)]
    scratch = scratch + [pltpu.VMEM((tm, 1), F32)]
    res = pl.pallas_call(
        functools.partial(_up_kernel, dec=dec, n_host_steps=grid[0] * grid[1], cast_w=cast_w),
        grid_spec=pltpu.PrefetchScalarGridSpec(num_scalar_prefetch=1, grid=grid, in_specs=in_specs,
                                               out_specs=out_specs, scratch_shapes=scratch),
        out_shape=out_shape,
        compiler_params=_params("arbitrary", "arbitrary", vmem_limit=VMEM_LIMIT_HOSTING),
    )(page_table, *args)
    return (res[1], res[0]) if dec is not None else (res[0], None)


def _down_kernel(pt_ref, a_ref, w_ref, x_ref, *rest, dec, n_host_steps):
    kk = pl.program_id(2)
    n_in = len(dec.block_shapes) if dec is not None else 0
    o_ref = rest[n_in + 1] if dec is not None else rest[0]

    @pl.when(kk == 0)
    def _():
        o_ref[...] = x_ref[...]

    def piece(rows):
        o_ref[rows, :] += jnp.dot(a_ref[rows, :], w_ref[...], preferred_element_type=F32)

    if dec is None:
        piece(pl.ds(0, o_ref.shape[0]))
    else:
        refs = rest[:n_in] + (rest[n_in],) + rest[n_in + 2:]
        lin = (pl.program_id(0) * pl.num_programs(1) + pl.program_id(1)) * pl.num_programs(2) + kk
        dec.run(lin, refs, n_host_steps, _row_pieces(o_ref.shape[0], piece))


def _down_proj(a, w_down, x, page_table, dec, tm, tn, tk):
    m, f = a.shape
    d = w_down.shape[1]
    grid = (m // tm, d // tn, f // tk)
    in_specs = [pl.BlockSpec((tm, tk), lambda i, j, k, pt: (i, k)),
                pl.BlockSpec((tk, tn), lambda i, j, k, pt: (k, j)),
                pl.BlockSpec((tm, tn), lambda i, j, k, pt: (i, j))]
    out_specs = [pl.BlockSpec((tm, tn), lambda i, j, k, pt: (i, j))]
    out_shape = [jax.ShapeDtypeStruct((m, d), F32)]
    args, scratch = [a, w_down, x], []
    n_host = grid[0] * grid[1] * grid[2]
    if dec is not None:
        assert n_host >= dec.n_lin
        d_in, d_out = dec.specs(lambda i, j, k: (i * grid[1] + j) * grid[2] + k)
        in_specs += d_in
        out_specs = [d_out] + out_specs
        out_shape = [dec.out_shape] + out_shape
        args += dec.args
        scratch = dec.scratch
    res = pl.pallas_call(
        functools.partial(_down_kernel, dec=dec, n_host_steps=n_host),
        grid_spec=pltpu.PrefetchScalarGridSpec(num_scalar_prefetch=1, grid=grid, in_specs=in_specs,
                                               out_specs=out_specs, scratch_shapes=scratch),
        out_shape=out_shape,
        compiler_params=_params("arbitrary", "arbitrary", "arbitrary"),
    )(page_table, *args)
    return (res[1], res[0]) if dec is not None else (res[0], None)


def _layer(xp, xs, ck, cvv, st, page_table, w, layer, slopes):
    bsz, s, d = xp.shape
    db, t, _ = xs.shape
    mp, ms = bsz * s, db * t
    att_w = d // 2
    conv_ch = d - att_w
    nh = att_w // V_DIM
    hist = CONV_WIDTH - 1
    lam_init = 0.8 - 0.6 * math.exp(-0.3 * layer)
    lam_vecs, sg = w['lam_vecs'], w['subln_g']
    n_pages = page_table.shape[1]
    tmp, tms = _tile(mp, 1024), _tile(ms, 1024)
    assert tms == ms

    xs2 = xs.reshape(ms, d)
    hs = _rms_cast(xs2, w['attn_norm_g'], _tile(ms, 256))
    offs = (0, att_w, 2 * att_w, 3 * att_w, 3 * att_w + conv_ch)
    (qs, ks_f, _, vs_f, _, us), w_in_pieces = _in_proj(hs, w, [(w['w_in'], o) for o in offs], tms, 512, True)
    r4 = lambda a: a.reshape(db, t, nh, V_DIM)
    nb_a = db // 2
    mk_dec = lambda b0, nb: _Decode(r4(qs), r4(ks_f), r4(vs_f), ck, cvv, slopes, lam_vecs, sg, lam_init,
                                    n_pages, b0, nb)
    w_out_b, w_down_b = _cast_bf16(w['w_out']), _cast_bf16(w['w_down'])

    xp2 = xp.reshape(mp, d)
    hp = _rms_cast(xp2, w['attn_norm_g'], _tile(mp, 256))
    (qp, kp_f, kp_b, vp_f, vp_b, upr), _ = _in_proj(hp, w, [(p, 0) for p in w_in_pieces], tmp, 1024, False)
    tq = _tile(s, 512)
    r3 = lambda a: a.reshape(bsz, s, att_w)
    att_p = _prompt_attention(r3(qp), r3(kp_b), r3(vp_b), slopes, lam_vecs, sg, lam_init, tq).reshape(mp, att_w)
    up3 = upr.reshape(bsz, s, conv_ch)
    conv_tt = _tile(s, 128)
    cv_p = _conv_module(up3, up3, conv_tt // HALO, True, w, conv_tt).reshape(mp, conv_ch)
    x1p, xgp, ssq_p = _out_proj(att_p, cv_p, w_out_b, xp2, w['mlp_norm_g'], tmp, 1024)
    act_p, att_a = _up_proj(xgp, ssq_p, w['w_up'], page_table, mk_dec(0, nb_a), tmp, 512)
    yp, att_b = _down_proj(act_p, w_down_b, x1p, page_table, mk_dec(nb_a, db - nb_a), tmp, 1024,
                           _tile(w_down_b.shape[0], 2048))

    att_s = jnp.concatenate([att_a, att_b], axis=0).reshape(ms, att_w)
    us3 = us.reshape(db, t, conv_ch)
    halo_s = jnp.pad(st, ((0, 0), (HALO - hist, 0), (0, 0)))
    cv_s = _conv_module(us3, halo_s, 0, False, w, t).reshape(ms, conv_ch)
    x1s, xgs, ssq_s = _out_proj(att_s, cv_s, w_out_b, xs2, w['mlp_norm_g'], tms, 1024)
    act_s, _ = _up_proj(xgs, ssq_s, w['w_up'], page_table, None, tms, 512)
    ys, _ = _down_proj(act_s, w_down_b, x1s, page_table, None, tms, 1024, _tile(w_down_b.shape[0], 4096))

    r5 = lambda a, b_, t_: a.reshape(b_, t_, nh, V_DIM)
    cs = jnp.concatenate([st, us3], axis=1)[:, -hist:]
    return (yp.reshape(bsz, s, d), ys.reshape(db, t, d),
            (r5(kp_f, bsz, s), r5(vp_f, bsz, s), up3[:, -hist:], r5(ks_f, db, t), r5(vs_f, db, t), cs))


def kernel(x_prompt, x_sample, cache_k, cache_v, state_conv, page_table, attn_norm_g, w_in, q_norm_g, k_norm_g, lambda_q1, lambda_k1, lambda_q2, lambda_k2, subln_g, conv_w, conv_b, conv_ln_g, conv_ln_b, w_out, mlp_norm_g, w_up, w_down):
    depth = w_in.shape[0]
    d = x_prompt.shape[-1]
    att_w = d // 2
    nh = att_w // V_DIM
    slopes = jnp.exp2(-8.0 * jnp.arange(1, nh + 1, dtype=F32) / nh)
    xp, xs = x_prompt, x_sample
    outs = [[] for _ in range(6)]
    for l in range(depth):
        w = {'attn_norm_g': attn_norm_g[l], 'w_in': w_in[l],
             'q_gain': jnp.tile(q_norm_g[l], att_w // HEAD_DIM).reshape(1, att_w),
             'k_gain': jnp.tile(k_norm_g[l], att_w // HEAD_DIM).reshape(1, att_w),
             'lam_vecs': [a[l].reshape(1, HEAD_DIM) for a in (lambda_q1, lambda_k1, lambda_q2, lambda_k2)],
             'subln_g': subln_g[l], 'conv_w': conv_w[l], 'conv_b': conv_b[l], 'conv_ln_g': conv_ln_g[l],
             'conv_ln_b': conv_ln_b[l], 'w_out': w_out[l], 'mlp_norm_g': mlp_norm_g[l], 'w_up': w_up[l],
             'w_down': w_down[l]}
        xp, xs, extra = _layer(xp, xs, cache_k[l], cache_v[l], state_conv[l], page_table, w, l, slopes)
        for lst, val in zip(outs, extra):
            lst.append(val)
    return (xp, xs) + tuple(jnp.stack(o) for o in outs)
```
